```python
import jax
import jax.numpy as jnp
from jax import lax
import numpy as np

D_MODEL = 4096
BATCH = 2
SEQ = 8192
DEPTH = 4

GRID_W = 64
CTX_LEN = 256
N_MIXERS = 2
N_NA_LAYERS = (DEPTH + 1) // 2
N_GLA_LAYERS = DEPTH // 2

NA_HEADS = 32
NA_HEAD_DIM = D_MODEL // NA_HEADS
NA_KH = 8
NA_KW = 16

GLA_HEADS = 8
GLA_DK = D_MODEL // 2 // GLA_HEADS
GLA_DV = D_MODEL // GLA_HEADS
GLA_DK_TOTAL = GLA_HEADS * GLA_DK
GLA_GATE_RANK = 16
GLA_GATE_NORMALIZER = 16.0
GLA_CHUNK = 64
ROPE_BASE = 10000.0

N_EXPERTS = 64
EXPERT_DIM = 192
TOP_K = 8
N_GROUPS = 8
TOPK_GROUPS = 4
ROUTED_SCALE = 2.5

ADA_RANK = 256
LN_EPS = 1e-6
DEEPNORM_ALPHA = (2 * DEPTH) ** 0.25
DEEPNORM_BETA = (8 * DEPTH) ** -0.25
NEG_INF = -1e30

kernel_name = "hybrid_na_gla_moe_diffusion_trunk"


def layernorm(x, g, b):
    xf = x.astype(jnp.float32)
    mu = jnp.mean(xf, axis=-1, keepdims=True)
    var = jnp.mean(jnp.square(xf - mu), axis=-1, keepdims=True)
    y = (xf - mu) * lax.rsqrt(var + LN_EPS)
    return (y * g + b).astype(x.dtype)


def adaln(cond, w1, w2, b):
    m = (jax.nn.silu(cond) @ w1) @ w2 + b
    return jnp.split(m, 6, axis=-1)


def modulate(x, shift, scale):
    return x * (1.0 + scale) + shift


def axial_rope(t, n_tokens):
    half = t.shape[-1] // 2
    nf = half // 2
    pos = jnp.arange(n_tokens)
    inv = ROPE_BASE ** (-jnp.arange(nf, dtype=jnp.float32) / nf)

    def rot(u, p):
        ang = p.astype(jnp.float32)[:, None] * inv
        cos = jnp.cos(ang)[None, :, None, :]
        sin = jnp.sin(ang)[None, :, None, :]
        u1 = u[..., :nf].astype(jnp.float32)
        u2 = u[..., nf:].astype(jnp.float32)
        return jnp.concatenate([u1 * cos - u2 * sin, u1 * sin + u2 * cos], axis=-1)

    return jnp.concatenate([rot(t[..., :half], pos // GRID_W), rot(t[..., half:], pos % GRID_W)], axis=-1)


def neighbourhood_attention(h_lat, h_ctx, w_qkv, w_o, rpb, need_ctx):
    B, S, _ = h_lat.shape
    n_ctx = h_ctx.shape[1]
    rows = S // GRID_W
    kh = min(NA_KH, rows)
    scale = NA_HEAD_DIM ** -0.5
    qkv = (h_lat @ w_qkv).reshape(B, rows, GRID_W, 3, NA_HEADS, NA_HEAD_DIM)
    q, k, v = qkv[..., 0, :, :], qkv[..., 1, :, :], qkv[..., 2, :, :]
    qkv_c = (h_ctx @ w_qkv).reshape(B, n_ctx, 3, NA_HEADS, NA_HEAD_DIM)
    qc, kc, vc = qkv_c[:, :, 0], qkv_c[:, :, 1], qkv_c[:, :, 2]

    col = jnp.arange(GRID_W)
    c_start = jnp.clip(col - NA_KW // 2, 0, GRID_W - NA_KW)
    col_ok = (col[None, :] >= c_start[:, None]) & (col[None, :] < c_start[:, None] + NA_KW)
    dc = jnp.clip(col[None, :] - col[:, None], 1 - NA_KW, NA_KW - 1) + NA_KW - 1
    band_mask = jnp.broadcast_to(col_ok[:, None, :], (GRID_W, kh, GRID_W)).reshape(GRID_W, kh * GRID_W)
    n_band = kh * GRID_W

    def row_block(r):
        r_start = jnp.clip(r - kh // 2, 0, rows - kh)
        dr = r_start + jnp.arange(kh) - r + NA_KH - 1
        bias = rpb[:, dr[None, :, None], dc[:, None, :]].reshape(NA_HEADS, GRID_W, n_band)
        q_r = q[:, r]
        k_b = lax.dynamic_slice_in_dim(k, r_start, kh, axis=1).reshape(B, n_band, NA_HEADS, NA_HEAD_DIM)
        v_b = lax.dynamic_slice_in_dim(v, r_start, kh, axis=1).reshape(B, n_band, NA_HEADS, NA_HEAD_DIM)
        s_band = jnp.einsum('bqhd,bkhd->bhqk', q_r, k_b).astype(jnp.float32) * scale + bias
        s_band = jnp.where(band_mask, s_band, NEG_INF)
        s_ctx = jnp.einsum('bqhd,bchd->bhqc', q_r, kc).astype(jnp.float32) * scale
        p = jax.nn.softmax(jnp.concatenate([s_band, s_ctx], axis=-1), axis=-1).astype(v.dtype)
        return (jnp.einsum('bhqk,bkhd->bqhd', p[..., :n_band], v_b)
                + jnp.einsum('bhqc,bchd->bqhd', p[..., n_band:], vc))

    o = lax.map(row_block, jnp.arange(rows))
    y_lat = o.transpose(1, 0, 2, 3, 4).reshape(B, S, D_MODEL) @ w_o
    if not need_ctx:
        return y_lat, None
    s = jnp.einsum('bqhd,bkhd->bhqk', qc, kc).astype(jnp.float32) * scale
    p = jax.nn.softmax(s, axis=-1).astype(vc.dtype)
    y_ctx = jnp.einsum('bhqk,bkhd->bqhd', p, vc).reshape(B, n_ctx, D_MODEL) @ w_o
    return y_lat, y_ctx


def gla_scan(q, k, v, log_a, s0):
    B, T, H, dk = q.shape
    dv = v.shape[-1]
    L = GLA_CHUNK
    n = T // L

    def chunks(a):
        return a.astype(jnp.float32).reshape(B, n, L, H, a.shape[-1]).transpose(1, 0, 3, 2, 4)

    tril = jnp.tril(jnp.ones((L, L), dtype=bool))

    def step(S, inp):
        qc, kc, vc, lac = inp
        b = jnp.cumsum(lac, axis=-2)
        m = b[..., L // 2:L // 2 + 1, :]
        bl = b[..., -1:, :]
        attn = jnp.einsum('bhtd,bhsd->bhts', qc * jnp.exp(b - m), kc * jnp.exp(m - b))
        attn = jnp.where(tril, attn, 0.0)
        o = (jnp.einsum('bhts,bhsv->bhtv', attn, vc)
             + jnp.einsum('bhtd,bhdv->bhtv', qc * jnp.exp(b), S))
        S = jnp.exp(bl[..., 0, :])[..., None] * S + jnp.einsum('bhsd,bhsv->bhdv', kc * jnp.exp(bl - b), vc)
        return S, o

    S_fin, o = lax.scan(step, s0, (chunks(q), chunks(k), chunks(v), chunks(log_a)))
    o = o.transpose(1, 0, 3, 2, 4).reshape(B, T, H, dv)
    return o, S_fin


def gla_mixer(h_lat, h_ctx, w_in, w_a1, w_a2, b_a, norm_w, w_o, need_ctx):
    def project(h):
        B, T, _ = h.shape
        z = h @ w_in
        q, k, v, g = jnp.split(z, [GLA_DK_TOTAL, 2 * GLA_DK_TOTAL, 2 * GLA_DK_TOTAL + D_MODEL], axis=-1)
        q = q.reshape(B, T, GLA_HEADS, GLA_DK) * (GLA_DK ** -0.5)
        k = k.reshape(B, T, GLA_HEADS, GLA_DK)
        v = v.reshape(B, T, GLA_HEADS, GLA_DV)
        la = [(jax.nn.log_sigmoid(((h @ w_a1[d]) @ w_a2[d] + b_a[d]).astype(jnp.float32))
               / GLA_GATE_NORMALIZER).reshape(B, T, GLA_HEADS, GLA_DK) for d in range(2)]
        return q, k, v, g, la[0], la[1]

    def output(o_f, o_b_rev, g):
        B, T = g.shape[:2]
        o = o_f + o_b_rev[:, ::-1]
        o = o * lax.rsqrt(jnp.mean(jnp.square(o), axis=-1, keepdims=True) + LN_EPS) * norm_w
        o = o.astype(g.dtype) * jax.nn.silu(g).reshape(B, T, GLA_HEADS, GLA_DV)
        return o.reshape(B, T, D_MODEL) @ w_o

    B, S, _ = h_lat.shape
    qc, kc, vc, gc, lac_f, lac_b = project(h_ctx)
    ql, kl, vl, gl, lal_f, lal_b = project(h_lat)
    ql = axial_rope(ql, S)
    kl = axial_rope(kl, S)
    s0 = jnp.zeros((B, GLA_HEADS, GLA_DK, GLA_DV), jnp.float32)
    oc_f, sc_f = gla_scan(qc, kc, vc, lac_f, s0)
    oc_b, sc_b = gla_scan(qc[:, ::-1], kc[:, ::-1], vc[:, ::-1], lac_b[:, ::-1], s0)
    ol_f, _ = gla_scan(ql, kl, vl, lal_f, sc_f)
    ol_b, _ = gla_scan(ql[:, ::-1], kl[:, ::-1], vl[:, ::-1], lal_b[:, ::-1], sc_b)
    y_lat = output(ol_f, ol_b, gl)
    if not need_ctx:
        return y_lat, None
    return y_lat, output(oc_f, oc_b, gc)


def moe_ffn(h, w_router, e_bias, w_gate, w_up, w_down, s_gate, s_up, s_down):
    shape = h.shape
    t = h.reshape(-1, D_MODEL)
    n = t.shape[0]
    scores = jax.nn.sigmoid((t @ w_router).astype(jnp.float32))
    biased = scores + e_bias.astype(jnp.float32)
    grp = biased.reshape(n, N_GROUPS, N_EXPERTS // N_GROUPS)
    grp_score = jnp.sum(lax.top_k(grp, 2)[0], axis=-1)
    _, top_g = lax.top_k(grp_score, TOPK_GROUPS)
    g_mask = jnp.sum(jax.nn.one_hot(top_g, N_GROUPS, dtype=jnp.float32), axis=1) > 0
    e_mask = jnp.repeat(g_mask, N_EXPERTS // N_GROUPS, axis=1)
    _, idx = lax.top_k(jnp.where(e_mask, biased, NEG_INF), TOP_K)
    w = jnp.take_along_axis(scores, idx, axis=1)
    w = w / jnp.sum(w, axis=-1, keepdims=True) * ROUTED_SCALE
    gates = jnp.sum(jax.nn.one_hot(idx, N_EXPERTS, dtype=jnp.float32) * w[..., None], axis=1).astype(t.dtype)
    shared = (jax.nn.silu(t @ s_gate) * (t @ s_up)) @ s_down

    def expert(acc, p):
        wg, wu, wd, g = p
        hid = jax.nn.silu(t @ wg) * (t @ wu) * g[:, None]
        return acc + hid @ wd, None

    out, _ = lax.scan(expert, shared, (w_gate, w_up, w_down, gates.T))
    return out.reshape(shape)


def setup_inputs(seed: int = 0) -> dict:
    key = jax.random.key(seed)
    ks = jax.random.split(key, 32)
    D = D_MODEL

    def nrm(k, shape, scale):
        return jax.random.normal(k, shape, jnp.float32) * scale

    return {
        "x": nrm(ks[0], (BATCH, SEQ, D), 1.0),
        "c": nrm(ks[1], (BATCH, D), 1.0),
        "ctx": nrm(ks[2], (BATCH, CTX_LEN, D), 1.0),
        "c_ctx": nrm(ks[3], (D,), 1.0),
        "ada_w1": nrm(ks[4], (DEPTH, D, ADA_RANK), D ** -0.5),
        "ada_w2": nrm(ks[5], (DEPTH, ADA_RANK, 6 * D), 0.5 * ADA_RANK ** -0.5),
        "ada_b": nrm(ks[6], (DEPTH, 6 * D), 0.01),
        "ln_g": 1.0 + nrm(ks[7], (DEPTH, 2, D), 0.01),
        "ln_b": nrm(ks[8], (DEPTH, 2, D), 0.01),
        "na_w_qkv": nrm(ks[9], (N_NA_LAYERS, D, 3 * D), D ** -0.5),
        "na_w_o": nrm(ks[10], (N_NA_LAYERS, D, D), DEEPNORM_BETA * D ** -0.5),
        "na_rpb": nrm(ks[11], (N_NA_LAYERS, NA_HEADS, 2 * NA_KH - 1, 2 * NA_KW - 1), 0.1),
        "gla_w_in": nrm(ks[12], (N_GLA_LAYERS, D, 2 * GLA_DK_TOTAL + 2 * D), D ** -0.5),
        "gla_w_a1": nrm(ks[13], (N_GLA_LAYERS, 2, D, GLA_GATE_RANK), D ** -0.5),
        "gla_w_a2": nrm(ks[14], (N_GLA_LAYERS, 2, GLA_GATE_RANK, GLA_DK_TOTAL), GLA_GATE_RANK ** -0.5),
        "gla_b_a": 1.0 + nrm(ks[15], (N_GLA_LAYERS, 2, GLA_DK_TOTAL), 0.5),
        "gla_norm": 1.0 + nrm(ks[16], (N_GLA_LAYERS, GLA_DV), 0.01),
        "gla_w_o": nrm(ks[17], (N_GLA_LAYERS, D, D), DEEPNORM_BETA * D ** -0.5),
        "moe_router": nrm(ks[18], (DEPTH, D, N_EXPERTS), D ** -0.5),
        "moe_bias": nrm(ks[19], (DEPTH, N_EXPERTS), 0.01),
        "moe_w_gate": nrm(ks[20], (DEPTH, N_EXPERTS, D, EXPERT_DIM), D ** -0.5),
        "moe_w_up": nrm(ks[21], (DEPTH, N_EXPERTS, D, EXPERT_DIM), D ** -0.5),
        "moe_w_down": nrm(ks[22], (DEPTH, N_EXPERTS, EXPERT_DIM, D), DEEPNORM_BETA * EXPERT_DIM ** -0.5),
        "sh_w_gate": nrm(ks[23], (DEPTH, D, EXPERT_DIM), D ** -0.5),
        "sh_w_up": nrm(ks[24], (DEPTH, D, EXPERT_DIM), D ** -0.5),
        "sh_w_down": nrm(ks[25], (DEPTH, EXPERT_DIM, D), DEEPNORM_BETA * EXPERT_DIM ** -0.5),
    }


def reference(x, c, ctx, c_ctx, ada_w1, ada_w2, ada_b, ln_g, ln_b, na_w_qkv, na_w_o, na_rpb,
              gla_w_in, gla_w_a1, gla_w_a2, gla_b_a, gla_norm, gla_w_o, moe_router, moe_bias,
              moe_w_gate, moe_w_up, moe_w_down, sh_w_gate, sh_w_up, sh_w_down):
    x_lat, x_ctx = x, ctx
    n_ctx = ctx.shape[1]
    for i in range(DEPTH):
        last = i == DEPTH - 1
        sh_m, sc_m, g_m, sh_f, sc_f, g_f = [m[:, None, :] for m in adaln(c, ada_w1[i], ada_w2[i], ada_b[i])]
        csh_m, csc_m, cg_m, csh_f, csc_f, cg_f = adaln(c_ctx, ada_w1[i], ada_w2[i], ada_b[i])
        h_lat = modulate(x_lat, sh_m, sc_m)
        h_ctx = modulate(x_ctx, csh_m, csc_m)
        j = i // N_MIXERS
        if i % N_MIXERS == 0:
            y_lat, y_ctx = neighbourhood_attention(h_lat, h_ctx, na_w_qkv[j], na_w_o[j], na_rpb[j], not last)
        else:
            y_lat, y_ctx = gla_mixer(h_lat, h_ctx, gla_w_in[j], gla_w_a1[j], gla_w_a2[j], gla_b_a[j],
                                     gla_norm[j], gla_w_o[j], not last)
        x_lat = layernorm(DEEPNORM_ALPHA * x_lat + g_m * y_lat, ln_g[i, 0], ln_b[i, 0])
        moe_args = (moe_router[i], moe_bias[i], moe_w_gate[i], moe_w_up[i], moe_w_down[i],
                    sh_w_gate[i], sh_w_up[i], sh_w_down[i])
        if last:
            y_lat = moe_ffn(modulate(x_lat, sh_f, sc_f), *moe_args)
        else:
            x_ctx = layernorm(DEEPNORM_ALPHA * x_ctx + cg_m * y_ctx, ln_g[i, 0], ln_b[i, 0])
            h = jnp.concatenate([modulate(x_ctx, csh_f, csc_f), modulate(x_lat, sh_f, sc_f)], axis=1)
            y = moe_ffn(h, *moe_args)
            y_ctx, y_lat = y[:, :n_ctx], y[:, n_ctx:]
            x_ctx = layernorm(DEEPNORM_ALPHA * x_ctx + cg_f * y_ctx, ln_g[i, 1], ln_b[i, 1])
        x_lat = layernorm(DEEPNORM_ALPHA * x_lat + g_f * y_lat, ln_g[i, 1], ln_b[i, 1])
    return x_lat
```

```python
import functools

import jax
import jax.numpy as jnp
import numpy as np
from jax import lax
from jax.experimental import pallas as pl
from jax.experimental.pallas import tpu as pltpu

GRID_W = 64
NA_HEADS = 32
NA_KH = 8
NA_KW = 16
GLA_HEADS = 8
GLA_GATE_NORMALIZER = 16.0
GLA_CHUNK = 64
ROPE_BASE = 10000.0
N_GROUPS = 8
TOPK_GROUPS = 4
TOP_K = 8
ROUTED_SCALE = 2.5
LN_EPS = 1e-6
NEG_INF = -1e30

LANES = 128
TM = 256
TG = 512
VMEM_LIMIT = 56 * 1024 * 1024

F32 = jnp.float32
BF16 = jnp.bfloat16
U32 = jnp.uint32
HI = lax.Precision.HIGHEST


def _cp(*sem):
    return pltpu.CompilerParams(dimension_semantics=sem, vmem_limit_bytes=VMEM_LIMIT)


def _sigmoid(x):
    return 1.0 / (1.0 + jnp.exp(-x))


def _silu(x):
    return x * _sigmoid(x)


def _dot(a, b):
    return jnp.dot(a, b, preferred_element_type=F32)


def _dot_nt(a, b):
    return lax.dot_general(a, b, (((1,), (1,)), ((), ())), preferred_element_type=F32)


def _pack_rows(h):
    w = h.shape[1] // 2
    lo = lax.bitcast_convert_type(h[:, :w].astype(BF16).astype(F32), U32)
    hi = lax.bitcast_convert_type(h[:, w:].astype(BF16).astype(F32), U32)
    return (hi & jnp.uint32(0xFFFF0000)) | (lo >> 16)


def _unpack_rows(p):
    lo = lax.bitcast_convert_type(p << 16, F32)
    hi = lax.bitcast_convert_type(p & jnp.uint32(0xFFFF0000), F32)
    return lo, hi


def _ln_mod(v, lng, lnb, shift, scale):
    mu = jnp.mean(v, axis=-1, keepdims=True)
    d = v - mu
    var = jnp.mean(d * d, axis=-1, keepdims=True)
    xn = d * lax.rsqrt(var + LN_EPS) * lng + lnb
    return xn, xn * (1.0 + scale) + shift


def _adaln_kernel(c_ref, w1_ref, w2_ref, b_ref, o_ref):
    a = _silu(c_ref[...])
    t = jnp.dot(a, w1_ref[0], precision=HI, preferred_element_type=F32)
    o_ref[0] = jnp.dot(t, w2_ref[0], precision=HI, preferred_element_type=F32) + b_ref[0]


def _adaln(cond, w1, w2, b):
    depth, d, r = w1.shape
    n6 = w2.shape[2]
    tn = min(n6, 2048)
    rows = cond.shape[0]
    return pl.pallas_call(
        _adaln_kernel,
        grid=(depth, n6 // tn),
        in_specs=[
            pl.BlockSpec((rows, d), lambda l, j: (0, 0)),
            pl.BlockSpec((1, d, r), lambda l, j: (l, 0, 0)),
            pl.BlockSpec((1, r, tn), lambda l, j: (l, 0, j)),
            pl.BlockSpec((1, 1, tn), lambda l, j: (l, 0, j)),
        ],
        out_specs=pl.BlockSpec((1, rows, tn), lambda l, j: (l, 0, j)),
        out_shape=jax.ShapeDtypeStruct((depth, rows, n6), F32),
        compiler_params=_cp("arbitrary", "arbitrary"),
        name="adaln",
    )(cond, w1, w2, b.reshape(depth, 1, n6))


def _seg_fn(nt, nl):
    return lambda i: 2 * (i // nt) + (i % nt) // nl


def _modulate_kernel(x_ref, sh_ref, sc_ref, h_ref):
    h_ref[...] = (x_ref[...] * (1.0 + sc_ref[0]) + sh_ref[0]).astype(h_ref.dtype)


def _modulate(x, shift, scale, seg):
    n, d = x.shape
    row = pl.BlockSpec((TM, d), lambda i: (i, 0))
    vec = pl.BlockSpec((1, 1, d), lambda i: (seg(i), 0, 0))
    return pl.pallas_call(
        _modulate_kernel,
        grid=(n // TM,),
        in_specs=[row, vec, vec],
        out_specs=row,
        out_shape=jax.ShapeDtypeStruct((n, d), BF16),
        compiler_params=_cp("arbitrary"),
        name="modulate",
    )(x, shift, scale)


def _resid_ln_kernel(alpha, packed, x_ref, y_ref, g_ref, lng_ref, lnb_ref, sh_ref, sc_ref, xo_ref, h_ref):
    v = alpha * x_ref[...] + g_ref[0] * y_ref[...].astype(F32)
    xn, h = _ln_mod(v, lng_ref[0], lnb_ref[0], sh_ref[0], sc_ref[0])
    xo_ref[...] = xn
    h_ref[...] = _pack_rows(h) if packed else h.astype(h_ref.dtype)


def _resid_ln(x, y, gate, lng, lnb, shift, scale, seg, alpha, packed):
    n, d = x.shape
    row = pl.BlockSpec((TM, d), lambda i: (i, 0))
    vec = pl.BlockSpec((1, 1, d), lambda i: (seg(i), 0, 0))
    one = pl.BlockSpec((1, 1, d), lambda i: (0, 0, 0))
    if packed:
        h_spec, h_shape = pl.BlockSpec((TM, d // 2), lambda i: (i, 0)), jax.ShapeDtypeStruct((n, d // 2), U32)
    else:
        h_spec, h_shape = row, jax.ShapeDtypeStruct((n, d), BF16)
    return pl.pallas_call(
        functools.partial(_resid_ln_kernel, alpha, packed),
        grid=(n // TM,),
        in_specs=[row, row, vec, one, one, vec, vec],
        out_specs=[row, h_spec],
        out_shape=[jax.ShapeDtypeStruct((n, d), F32), h_shape],
        compiler_params=_cp("arbitrary"),
        name="resid_ln",
    )(x, y, gate, lng, lnb, shift, scale)


def _mm_kernel(x_ref, w_ref, o_ref, wbf_ref):
    @pl.when(pl.program_id(1) == 0)
    def _():
        wbf_ref[...] = w_ref[0].astype(BF16)

    o_ref[...] = _dot(x_ref[...], wbf_ref[...]).astype(o_ref.dtype)


def _mm(x, w, layer, out_dtype, tn):
    n, k = x.shape
    m = w.shape[2]
    tn = min(tn, m)
    return pl.pallas_call(
        _mm_kernel,
        grid=(m // tn, n // TM),
        in_specs=[
            pl.BlockSpec((TM, k), lambda j, i: (i, 0)),
            pl.BlockSpec((1, k, tn), lambda j, i: (layer, 0, j)),
        ],
        out_specs=pl.BlockSpec((TM, tn), lambda j, i: (i, j)),
        out_shape=jax.ShapeDtypeStruct((n, m), out_dtype),
        scratch_shapes=[pltpu.VMEM((k, tn), BF16)],
        compiler_params=_cp("arbitrary", "arbitrary"),
        name="mm",
    )(x, w)


NA_HG = 2


def _na_bias_tables(rpb):
    col = np.arange(GRID_W)
    c_start = np.clip(col - NA_KW // 2, 0, GRID_W - NA_KW)
    col_ok = (col[None, :] >= c_start[:, None]) & (col[None, :] < c_start[:, None] + NA_KW)
    dc = np.clip(col[None, :] - col[:, None], 1 - NA_KW, NA_KW - 1) + NA_KW - 1
    delta = np.arange(NA_KH)
    dr = np.arange(NA_KH)[None, :] - delta[:, None] + NA_KH - 1
    t = rpb[:, dr[:, None, :, None], dc[None, :, None, :]]
    t = jnp.where(col_ok[None, None, :, None, :], t, NEG_INF)
    return t.reshape(rpb.shape[0], NA_KH, GRID_W, NA_KH * GRID_W)


def _na_kernel(s_lat, c_ctx, hd, q_ref, k_ref, v_ref, bias_ref, o_ref):
    rows = s_lat // GRID_W
    nband = NA_KH * GRID_W
    scale = hd ** -0.5
    for hh in range(NA_HG):
        ln = slice(hh * hd, (hh + 1) * hd)
        kc = k_ref[s_lat:s_lat + c_ctx, ln]
        vc = v_ref[s_lat:s_lat + c_ctx, ln]

        def row_body(r, carry, ln=ln, kc=kc, vc=vc, hh=hh):
            r_start = jnp.clip(r - NA_KH // 2, 0, rows - NA_KH)
            q_r = q_ref[pl.ds(pl.multiple_of(r * GRID_W, GRID_W), GRID_W), ln]
            b0 = pl.multiple_of(r_start * GRID_W, GRID_W)
            k_b = k_ref[pl.ds(b0, nband), ln]
            v_b = v_ref[pl.ds(b0, nband), ln]
            s_band = _dot_nt(q_r, k_b) * scale + bias_ref[hh, r - r_start]
            s_ctx = _dot_nt(q_r, kc) * scale
            m = jnp.maximum(jnp.max(s_band, axis=-1, keepdims=True), jnp.max(s_ctx, axis=-1, keepdims=True))
            p_band = jnp.exp(s_band - m)
            p_ctx = jnp.exp(s_ctx - m)
            den = jnp.sum(p_band, axis=-1, keepdims=True) + jnp.sum(p_ctx, axis=-1, keepdims=True)
            o = _dot(p_band.astype(BF16), v_b) + _dot(p_ctx.astype(BF16), vc)
            o_ref[pl.ds(pl.multiple_of(r * GRID_W, GRID_W), GRID_W), ln] = (o / den).astype(o_ref.dtype)
            return carry

        lax.fori_loop(0, rows, row_body, 0)
        qc = q_ref[s_lat:s_lat + c_ctx, ln]
        s = _dot_nt(qc, kc) * scale
        p = jnp.exp(s - jnp.max(s, axis=-1, keepdims=True))
        o = _dot(p.astype(BF16), vc) / jnp.sum(p, axis=-1, keepdims=True)
        o_ref[s_lat:s_lat + c_ctx, ln] = o.astype(o_ref.dtype)


def _na_attention(qkv, bias, batch, s_lat, c_ctx, d):
    r = s_lat + c_ctx
    hd = d // NA_HEADS
    wb = NA_HG * hd
    nhg = NA_HEADS // NA_HG
    return pl.pallas_call(
        functools.partial(_na_kernel, s_lat, c_ctx, hd),
        grid=(batch, nhg),
        in_specs=[
            pl.BlockSpec((r, wb), lambda b, g: (b, g)),
            pl.BlockSpec((r, wb), lambda b, g: (b, nhg + g)),
            pl.BlockSpec((r, wb), lambda b, g: (b, 2 * nhg + g)),
            pl.BlockSpec((NA_HG, NA_KH, GRID_W, NA_KH * GRID_W), lambda b, g: (g, 0, 0, 0)),
        ],
        out_specs=pl.BlockSpec((r, wb), lambda b, g: (b, g)),
        out_shape=jax.ShapeDtypeStruct((batch * r, d), BF16),
        compiler_params=_cp("arbitrary", "arbitrary"),
        name="na_attention",
    )(qkv, qkv, qkv, bias)


def _rope_tables(s_lat, c_ctx, dk):
    half = dk // 2
    nf = half // 2
    pos = np.arange(s_lat)
    inv = ROPE_BASE ** (-np.arange(nf, dtype=np.float32) / nf)
    ang_r = (pos // GRID_W).astype(np.float32)[:, None] * inv
    ang_c = (pos % GRID_W).astype(np.float32)[:, None] * inv
    cos = np.concatenate([np.cos(ang_r), np.cos(ang_r), np.cos(ang_c), np.cos(ang_c)], axis=1)
    sin = np.concatenate([-np.sin(ang_r), np.sin(ang_r), -np.sin(ang_c), np.sin(ang_c)], axis=1)
    cos = np.concatenate([cos, np.ones((c_ctx, dk), np.float32)], axis=0)
    sin = np.concatenate([sin, np.zeros((c_ctx, dk), np.float32)], axis=0)
    return jnp.asarray(cos, F32), jnp.asarray(sin, F32)


def _swap_quarters(u):
    parts = [pltpu.roll(u[:, i:i + LANES], LANES // 2, 1) for i in range(0, u.shape[1], LANES)]
    return jnp.concatenate(parts, axis=1)


def _gla_scan_kernel(dk, q_ref, k_ref, v_ref, r_ref, w2_ref, ba_ref, cos_ref, sin_ref, o_ref, st_ref):
    d = pl.program_id(2)
    L = GLA_CHUNK
    nch = q_ref.shape[0] // L

    @pl.when(pl.program_id(3) == 0)
    def _():
        st_ref[...] = jnp.zeros_like(st_ref)

    row = lax.broadcasted_iota(jnp.int32, (L, L), 0)
    col = lax.broadcasted_iota(jnp.int32, (L, L), 1)
    tri = jnp.where(d == 0, row - col, col - row) >= 0
    tri_f = jnp.where(tri, 1.0, 0.0)
    w2 = w2_ref[0, 0]
    ba = ba_ref[0, 0]

    for c in range(nch):
        cc = jnp.where(d == 0, c, nch - 1 - c)
        sl = pl.ds(pl.multiple_of(cc * L, L), L)
        cos = cos_ref[sl, :]
        sin = sin_ref[sl, :]
        qc = q_ref[sl, :].astype(F32) * (dk ** -0.5)
        kc = k_ref[sl, :].astype(F32)
        qc = qc * cos + _swap_quarters(qc) * sin
        kc = kc * cos + _swap_quarters(kc) * sin
        vc = v_ref[sl, :]
        pre = jnp.dot(r_ref[sl, :], w2, precision=HI, preferred_element_type=F32) + ba
        lac = (jnp.minimum(pre, 0.0) - jnp.log(1.0 + jnp.exp(-jnp.abs(pre)))) / GLA_GATE_NORMALIZER
        b = jnp.dot(tri_f, lac, precision=HI, preferred_element_type=F32)
        bl = jnp.sum(lac, axis=0, keepdims=True)
        m = b[L // 2:L // 2 + 1]
        attn = _dot_nt((qc * jnp.exp(b - m)).astype(BF16), (kc * jnp.exp(m - b)).astype(BF16))
        attn = jnp.where(tri, attn, 0.0)
        st_old = st_ref[...]
        o = _dot(attn.astype(BF16), vc) + _dot_nt((qc * jnp.exp(b)).astype(BF16), st_old.astype(BF16))
        o_ref[0, sl, :] = o.astype(o_ref.dtype)
        kd = (kc * jnp.exp(bl - b)).astype(BF16)
        upd = lax.dot_general(vc, kd, (((0,), (0,)), ((), ())), preferred_element_type=F32)
        st_ref[...] = jnp.exp(bl) * st_old + upd


def _gla_scan(z, r, w2pad, ba, cos, sin, batch, s_lat, c_ctx, d):
    n = z.shape[0]
    hg = GLA_HEADS
    dkt = d // 2
    dk, dv = dkt // hg, d // hg
    nl = s_lat // TM
    nt = nl + 1
    assert c_ctx == TM

    def blk(dr, s):
        return jnp.where(s == 0, nl, jnp.where(dr == 0, s - 1, nl - s))

    return pl.pallas_call(
        functools.partial(_gla_scan_kernel, dk),
        grid=(batch, hg, 2, nt),
        in_specs=[
            pl.BlockSpec((TM, dk), lambda b, h, dr, s: (b * nt + blk(dr, s), h)),
            pl.BlockSpec((TM, dk), lambda b, h, dr, s: (b * nt + blk(dr, s), hg + h)),
            pl.BlockSpec((TM, dv), lambda b, h, dr, s: (b * nt + blk(dr, s), 2 * dkt // dv + h)),
            pl.BlockSpec((TM, LANES), lambda b, h, dr, s: (b * nt + blk(dr, s), 0)),
            pl.BlockSpec((1, 1, LANES, dk), lambda b, h, dr, s: (dr, h, 0, 0)),
            pl.BlockSpec((1, 1, 1, dk), lambda b, h, dr, s: (dr, h, 0, 0)),
            pl.BlockSpec((TM, dk), lambda b, h, dr, s: (blk(dr, s), 0)),
            pl.BlockSpec((TM, dk), lambda b, h, dr, s: (blk(dr, s), 0)),
        ],
        out_specs=pl.BlockSpec((1, TM, dv), lambda b, h, dr, s: (dr, b * nt + blk(dr, s), h)),
        out_shape=jax.ShapeDtypeStruct((2, n, d), F32),
        scratch_shapes=[pltpu.VMEM((dv, dk), F32)],
        compiler_params=_cp("arbitrary", "arbitrary", "arbitrary", "arbitrary"),
        name="gla_scan",
    )(z, z, z, r, w2pad, ba, cos, sin)


def _gla_out_kernel(dv, o_ref, g_ref, nw_ref, h_ref):
    nw = nw_ref[0]
    for h in range(o_ref.shape[2] // dv):
        sl = slice(h * dv, (h + 1) * dv)
        o = o_ref[0, :, sl] + o_ref[1, :, sl]
        o = o * lax.rsqrt(jnp.mean(o * o, axis=-1, keepdims=True) + LN_EPS) * nw
        h_ref[:, sl] = (o * _silu(g_ref[:, sl].astype(F32))).astype(h_ref.dtype)


def _gla_out(o2, z, norm_w, layer, d):
    n = z.shape[0]
    dv = d // GLA_HEADS
    gcol = (z.shape[1] - d) // d
    return pl.pallas_call(
        functools.partial(_gla_out_kernel, dv),
        grid=(n // TM,),
        in_specs=[
            pl.BlockSpec((2, TM, d), lambda i: (0, i, 0)),
            pl.BlockSpec((TM, d), lambda i: (i, gcol)),
            pl.BlockSpec((1, 1, dv), lambda i: (layer, 0, 0)),
        ],
        out_specs=pl.BlockSpec((TM, d), lambda i: (i, 0)),
        out_shape=jax.ShapeDtypeStruct((n, d), BF16),
        compiler_params=_cp("arbitrary"),
        name="gla_out",
    )(o2, z, norm_w.reshape(norm_w.shape[0], 1, dv))


def _rank_desc(v, axis_len):
    idx = lax.broadcasted_iota(jnp.int32, v.shape, 0)
    rank = jnp.zeros(v.shape, jnp.int32)
    for j in range(axis_len):
        vj = v[j:j + 1]
        rank = rank + jnp.where((vj > v) | ((vj == v) & (j < idx)), 1, 0)
    return rank


def _router_kernel(n_e, x_ref, sh_ref, sc_ref, wr_ref, eb_ref, eid_ref, rnk_ref, gw_ref, cnt_ref, run_ref):
    @pl.when(pl.program_id(0) == 0)
    def _():
        run_ref[...] = jnp.zeros_like(run_ref)

    tm = x_ref.shape[0]
    gsz = n_e // N_GROUPS
    h = x_ref[...] * (1.0 + sc_ref[0]) + sh_ref[0]
    logits = lax.dot_general(wr_ref[0], h, (((1,), (1,)), ((), ())), precision=HI, preferred_element_type=F32)
    scores = _sigmoid(logits)
    biased = scores + eb_ref[0]
    grp = biased.reshape(N_GROUPS, gsz, tm)
    m1 = jnp.max(grp, axis=1, keepdims=True)
    gi = lax.broadcasted_iota(jnp.int32, grp.shape, 1)
    first = jnp.min(jnp.where(grp == m1, gi, gsz), axis=1, keepdims=True)
    m2 = jnp.max(jnp.where(gi == first, -jnp.inf, grp), axis=1, keepdims=True)
    g_sel = _rank_desc(m1 + m2, N_GROUPS) < TOPK_GROUPS
    e_sel = jnp.broadcast_to(g_sel, (N_GROUPS, gsz, tm)).reshape(n_e, tm)
    masked = jnp.where(e_sel, biased, NEG_INF)
    rank = _rank_desc(masked, n_e)
    sel = rank < TOP_K
    w = jnp.where(sel, scores, 0.0)
    gates = w / jnp.sum(w, axis=0, keepdims=True) * ROUTED_SCALE
    sel_f = jnp.where(sel, 1.0, 0.0)
    t0 = lax.broadcasted_iota(jnp.int32, (tm, tm), 0)
    t1 = lax.broadcasted_iota(jnp.int32, (tm, tm), 1)
    before = _dot(sel_f.astype(BF16), jnp.where(t0 < t1, 1.0, 0.0).astype(BF16))
    pos = run_ref[:, 0:1] + before
    run_ref[...] = run_ref[...] + jnp.sum(sel_f, axis=1, keepdims=True)
    cnt_ref[...] = run_ref[...].astype(jnp.int32)
    eidx = lax.broadcasted_iota(jnp.int32, (n_e, tm), 0).astype(F32)
    for j in range(TOP_K):
        hit = rank == j
        eid_ref[j:j + 1, :] = jnp.sum(jnp.where(hit, eidx, 0.0), axis=0, keepdims=True).astype(jnp.int32)
        rnk_ref[j:j + 1, :] = jnp.sum(jnp.where(hit, pos, 0.0), axis=0, keepdims=True).astype(jnp.int32)
        gw_ref[j:j + 1, :] = jnp.sum(jnp.where(hit, gates, 0.0), axis=0, keepdims=True)


def _router(x, shift, scale, w_router_t, e_bias, layer, seg):
    n, d = x.shape
    n_e = w_router_t.shape[1]
    vec = pl.BlockSpec((1, 1, d), lambda i: (seg(i), 0, 0))
    sel = pl.BlockSpec((TOP_K, TM), lambda i: (0, i))
    return pl.pallas_call(
        functools.partial(_router_kernel, n_e),
        grid=(n // TM,),
        in_specs=[
            pl.BlockSpec((TM, d), lambda i: (i, 0)),
            vec, vec,
            pl.BlockSpec((1, n_e, d), lambda i: (layer, 0, 0)),
            pl.BlockSpec((1, n_e, 1), lambda i: (layer, 0, 0)),
        ],
        out_specs=[sel, sel, sel, pl.BlockSpec((n_e, LANES), lambda i: (0, 0))],
        out_shape=[
            jax.ShapeDtypeStruct((TOP_K, n), jnp.int32),
            jax.ShapeDtypeStruct((TOP_K, n), jnp.int32),
            jax.ShapeDtypeStruct((TOP_K, n), F32),
            jax.ShapeDtypeStruct((n_e, LANES), jnp.int32),
        ],
        scratch_shapes=[pltpu.VMEM((n_e, LANES), F32)],
        compiler_params=_cp("arbitrary"),
        name="router",
    )(x, shift, scale, w_router_t, e_bias)


def _expert_kernel(te_ref, nt_ref, tok_ref, hp_ref, wg_ref, wu_ref, wd_ref, y_ref, xbuf, wgb, wub, wdb, sem):
    j = pl.program_id(0)
    tg = xbuf.shape[0]

    def row_copy(r):
        return pltpu.make_async_copy(hp_ref.at[pl.ds(tok_ref[0, 0, r], 1)], xbuf.at[pl.ds(r, 1)], sem)

    @pl.when(j < nt_ref[0])
    def _():
        def issue(r, c):
            row_copy(r).start()
            return c

        lax.fori_loop(0, tg, issue, 0)

        @pl.when((j == 0) | (te_ref[j] != te_ref[jnp.maximum(j - 1, 0)]))
        def _():
            wgb[...] = wg_ref[0, 0].astype(BF16)
            wub[...] = wu_ref[0, 0].astype(BF16)
            wdb[...] = wd_ref[0, 0].astype(BF16)

        def wait(r, c):
            row_copy(r).wait()
            return c

        lax.fori_loop(0, tg, wait, 0)
        lo, hi = _unpack_rows(xbuf[...])
        x = jnp.concatenate([lo.astype(BF16), hi.astype(BF16)], axis=1)
        hid = _silu(_dot(x, wgb[...])) * _dot(x, wub[...])
        y_ref[...] = _pack_rows(_dot(hid.astype(BF16), wdb[...]))

    @pl.when(j >= nt_ref[0])
    def _():
        y_ref[...] = jnp.zeros_like(y_ref)


def _experts(hp, tok, tile_eid, n_tiles_used, w_gate, w_up, w_down, layer):
    n, dh = hp.shape
    d = 2 * dh
    f = w_gate.shape[3]
    n_tiles = tok.shape[0]
    grid_spec = pltpu.PrefetchScalarGridSpec(
        num_scalar_prefetch=2,
        grid=(n_tiles,),
        in_specs=[
            pl.BlockSpec((1, 1, TG), lambda j, te, nt: (j, 0, 0), memory_space=pltpu.SMEM),
            pl.BlockSpec(memory_space=pl.ANY),
            pl.BlockSpec((1, 1, d, f), lambda j, te, nt: (layer, te[j], 0, 0)),
            pl.BlockSpec((1, 1, d, f), lambda j, te, nt: (layer, te[j], 0, 0)),
            pl.BlockSpec((1, 1, f, d), lambda j, te, nt: (layer, te[j], 0, 0)),
        ],
        out_specs=pl.BlockSpec((TG, dh), lambda j, te, nt: (j, 0)),
        scratch_shapes=[
            pltpu.VMEM((TG, dh), U32),
            pltpu.VMEM((d, f), BF16),
            pltpu.VMEM((d, f), BF16),
            pltpu.VMEM((f, d), BF16),
            pltpu.SemaphoreType.DMA(()),
        ],
    )
    return pl.pallas_call(
        _expert_kernel,
        grid_spec=grid_spec,
        out_shape=jax.ShapeDtypeStruct((n_tiles * TG, dh), U32),
        compiler_params=_cp("arbitrary"),
        name="experts",
    )(tile_eid, n_tiles_used, tok, hp, w_gate, w_up, w_down)


def _shared_kernel(hp_ref, sg_ref, su_ref, sd_ref, y_ref, sgb, sub, sdb):
    @pl.when(pl.program_id(0) == 0)
    def _():
        sgb[...] = sg_ref[0].astype(BF16)
        sub[...] = su_ref[0].astype(BF16)
        sdb[...] = sd_ref[0].astype(BF16)

    lo, hi = _unpack_rows(hp_ref[...])
    x = jnp.concatenate([lo.astype(BF16), hi.astype(BF16)], axis=1)
    hid = _silu(_dot(x, sgb[...])) * _dot(x, sub[...])
    y_ref[...] = _dot(hid.astype(BF16), sdb[...]).astype(y_ref.dtype)


def _shared_expert(hp, sg, su, sd, layer):
    n, dh = hp.shape
    d = 2 * dh
    f = sg.shape[2]
    return pl.pallas_call(
        _shared_kernel,
        grid=(n // TM,),
        in_specs=[
            pl.BlockSpec((TM, dh), lambda i: (i, 0)),
            pl.BlockSpec((1, d, f), lambda i: (layer, 0, 0)),
            pl.BlockSpec((1, d, f), lambda i: (layer, 0, 0)),
            pl.BlockSpec((1, f, d), lambda i: (layer, 0, 0)),
        ],
        out_specs=pl.BlockSpec((TM, d), lambda i: (i, 0)),
        out_shape=jax.ShapeDtypeStruct((n, d), BF16),
        scratch_shapes=[pltpu.VMEM((d, f), BF16), pltpu.VMEM((d, f), BF16), pltpu.VMEM((f, d), BF16)],
        compiler_params=_cp("arbitrary"),
        name="shared_expert",
    )(hp, sg, su, sd)


TMC = 128


def _combine_kernel(alpha, dest_ref, ys_ref, ysh_ref, gw_ref, x_ref, g_ref, lng_ref, lnb_ref, sh_ref, sc_ref,
                    xo_ref, h_ref, buf, sem):
    tm, d = x_ref.shape
    dh = d // 2

    def row_copy(i):
        return pltpu.make_async_copy(ys_ref.at[pl.ds(dest_ref[0, 0, i], 1)], buf.at[pl.ds(i, 1)], sem)

    def issue(i, c):
        row_copy(i).start()
        return c

    lax.fori_loop(0, TOP_K * tm, issue, 0)

    def wait(i, c):
        row_copy(i).wait()
        return c

    lax.fori_loop(0, TOP_K * tm, wait, 0)
    ysh = ysh_ref[...].astype(F32)
    y_lo, y_hi = ysh[:, :dh], ysh[:, dh:]
    for kk in range(TOP_K):
        lo, hi = _unpack_rows(buf[kk * tm:(kk + 1) * tm, :])
        w = gw_ref[:, kk:kk + 1]
        y_lo = y_lo + w * lo
        y_hi = y_hi + w * hi
    y = jnp.concatenate([y_lo, y_hi], axis=1)
    v = alpha * x_ref[...] + g_ref[0] * y
    xn, h = _ln_mod(v, lng_ref[0], lnb_ref[0], sh_ref[0], sc_ref[0])
    xo_ref[...] = xn
    h_ref[...] = h.astype(h_ref.dtype)


def _combine(dest, ys, ysh, gw, x, gate, lng, lnb, shift, scale, seg, alpha):
    n, d = x.shape
    row = pl.BlockSpec((TMC, d), lambda i: (i, 0))
    vec = pl.BlockSpec((1, 1, d), lambda i: (seg(i), 0, 0))
    one = pl.BlockSpec((1, 1, d), lambda i: (0, 0, 0))
    return pl.pallas_call(
        functools.partial(_combine_kernel, alpha),
        grid=(n // TMC,),
        in_specs=[
            pl.BlockSpec((1, 1, TOP_K * TMC), lambda i: (i, 0, 0), memory_space=pltpu.SMEM),
            pl.BlockSpec(memory_space=pl.ANY),
            row,
            pl.BlockSpec((TMC, TOP_K), lambda i: (i, 0)),
            row, vec, one, one, vec, vec,
        ],
        out_specs=[row, row],
        out_shape=[jax.ShapeDtypeStruct((n, d), F32), jax.ShapeDtypeStruct((n, d), BF16)],
        scratch_shapes=[pltpu.VMEM((TOP_K * TMC, d // 2), U32), pltpu.SemaphoreType.DMA(())],
        compiler_params=_cp("arbitrary"),
        name="moe_combine",
    )(dest, ys, ysh, gw, x, gate, lng, lnb, shift, scale)


def _moe(x, hp, mods_f, w_router_t, e_bias, w_gate, w_up, w_down, sg, su, sd, gate, lng, lnb, nshift, nscale,
         layer, seg, seg_c, alpha):
    n, d = x.shape
    n_e = w_router_t.shape[1]
    shift_f, scale_f = mods_f
    eid, rnk, gw, cnt = _router(x, shift_f, scale_f, w_router_t, e_bias, layer, seg)
    counts = cnt[:, 0]
    tiles_e = (counts + TG - 1) // TG
    tile_end = jnp.cumsum(tiles_e)
    n_tiles = (n * TOP_K) // TG + n_e
    tile_eid = jnp.minimum(jnp.searchsorted(tile_end, jnp.arange(n_tiles, dtype=jnp.int32), side="right"),
                           n_e - 1).astype(jnp.int32)
    row_off = (tile_end - tiles_e) * TG
    dest = row_off[eid] + rnk
    tok = jnp.zeros((n_tiles * TG,), jnp.int32).at[dest.reshape(-1)].set(
        jnp.broadcast_to(jnp.arange(n, dtype=jnp.int32), (TOP_K, n)).reshape(-1))
    ys = _experts(hp, tok.reshape(n_tiles, 1, TG), tile_eid, tile_end[-1:].astype(jnp.int32), w_gate, w_up, w_down,
                  layer)
    ysh = _shared_expert(hp, sg, su, sd, layer)
    dest_t = dest.reshape(TOP_K, n // TMC, TMC).transpose(1, 0, 2).reshape(n // TMC, 1, TOP_K * TMC)
    return _combine(dest_t, ys, ysh, gw.T, x, gate, lng, lnb, nshift, nscale, seg_c, alpha)


def kernel(x, c, ctx, c_ctx, ada_w1, ada_w2, ada_b, ln_g, ln_b, na_w_qkv, na_w_o, na_rpb, gla_w_in, gla_w_a1,
           gla_w_a2, gla_b_a, gla_norm, gla_w_o, moe_router, moe_bias, moe_w_gate, moe_w_up, moe_w_down,
           sh_w_gate, sh_w_up, sh_w_down):
    batch, s_lat, d = x.shape
    c_ctx_len = ctx.shape[1]
    depth = ada_w1.shape[0]
    alpha = (2 * depth) ** 0.25
    assert s_lat % TM == 0 and c_ctx_len == TM and s_lat % GRID_W == 0
    r = s_lat + c_ctx_len
    n = batch * r
    nl, nt = s_lat // TM, r // TM
    seg = _seg_fn(nt, nl)
    seg_c = _seg_fn(nt * (TM // TMC), nl * (TM // TMC))

    xa = jnp.concatenate([x, ctx], axis=1).reshape(n, d)
    cond = jnp.stack([c, jnp.broadcast_to(c_ctx, c.shape)], axis=1).reshape(2 * batch, d)
    rows = -(-2 * batch // 8) * 8
    cond = jnp.pad(cond, ((0, rows - 2 * batch), (0, 0)))
    mods = _adaln(cond, ada_w1, ada_w2, ada_b).reshape(depth, rows, 6, d).transpose(0, 2, 1, 3)
    mods = mods.reshape(depth, 6, rows, 1, d)
    lng = ln_g.reshape(depth, 2, 1, 1, d)
    lnb = ln_b.reshape(depth, 2, 1, 1, d)

    hd_gla = d // 2 // GLA_HEADS
    cos, sin = _rope_tables(s_lat, c_ctx_len, hd_gla)
    n_e = moe_router.shape[2]
    w_router_t = jnp.swapaxes(moe_router, 1, 2)
    e_bias = moe_bias.reshape(depth, n_e, 1)
    rank_g = gla_w_a1.shape[3]
    assert 2 * rank_g <= LANES

    h = _modulate(xa, mods[0, 0], mods[0, 1], seg)
    for i in range(depth):
        last = i == depth - 1
        m = mods[i]
        j = i // 2
        if i % 2 == 0:
            qkv = _mm(h, na_w_qkv, j, BF16, 512)
            o = _na_attention(qkv, _na_bias_tables(na_rpb[j]), batch, s_lat, c_ctx_len, d)
            y = _mm(o, na_w_o, j, F32, 512)
        else:
            z = _mm(h, gla_w_in, j, BF16, 512)
            w_a1 = jnp.concatenate([gla_w_a1[j, 0], gla_w_a1[j, 1]], axis=1)
            w_a1 = jnp.pad(w_a1, ((0, 0), (0, LANES - 2 * rank_g)))[None]
            rr = _mm(h, w_a1, 0, F32, LANES)
            w2 = gla_w_a2[j].reshape(2, rank_g, GLA_HEADS, hd_gla).transpose(0, 2, 1, 3)
            w2pad = jnp.zeros((2, GLA_HEADS, LANES, hd_gla), F32)
            w2pad = w2pad.at[0, :, :rank_g].set(w2[0]).at[1, :, rank_g:2 * rank_g].set(w2[1])
            ba = gla_b_a[j].reshape(2, GLA_HEADS, 1, hd_gla)
            o2 = _gla_scan(z, rr, w2pad, ba, cos, sin, batch, s_lat, c_ctx_len, d)
            og = _gla_out(o2, z, gla_norm, j, d)
            y = _mm(og, gla_w_o, j, F32, 512)
        x1, hp = _resid_ln(xa, y, m[2], lng[i, 0], lnb[i, 0], m[3], m[4], seg, alpha, packed=True)
        nxt = mods[i + 1] if not last else m
        xa, h = _moe(x1, hp, (m[3], m[4]), w_router_t, e_bias, moe_w_gate, moe_w_up, moe_w_down,
                     sh_w_gate, sh_w_up, sh_w_down, m[5], lng[i, 1], lnb[i, 1], nxt[0], nxt[1],
                     i, seg, seg_c, alpha)
    return xa.reshape(batch, r, d)[:, :s_lat]
```

```python
import functools

import jax
import jax.numpy as jnp
import numpy as np
from jax import lax
from jax.experimental import pallas as pl
from jax.experimental.pallas import tpu as pltpu

GRID_W = 64
NA_HEADS = 32
NA_KH = 8
NA_KW = 16
GLA_HEADS = 8
GLA_GATE_NORMALIZER = 16.0
GLA_CHUNK = 64
ROPE_BASE = 10000.0
N_GROUPS = 8
TOPK_GROUPS = 4
TOP_K = 8
ROUTED_SCALE = 2.5
LN_EPS = 1e-6
NEG_INF = -1e30

LANES = 128
TM = 256
TG = 512
VMEM_LIMIT = 56 * 1024 * 1024

F32 = jnp.float32
BF16 = jnp.bfloat16
U32 = jnp.uint32
HI = lax.Precision.HIGHEST


def _cp(*sem):
    return pltpu.CompilerParams(dimension_semantics=sem, vmem_limit_bytes=VMEM_LIMIT)


def _sigmoid(x):
    return 1.0 / (1.0 + jnp.exp(-x))


def _silu(x):
    return x * _sigmoid(x)


def _dot(a, b):
    return jnp.dot(a, b, preferred_element_type=F32)


def _dot_nt(a, b):
    return lax.dot_general(a, b, (((1,), (1,)), ((), ())), preferred_element_type=F32)


def _pack_rows(h):
    w = h.shape[1] // 2
    lo = lax.bitcast_convert_type(h[:, :w].astype(BF16).astype(F32), U32)
    hi = lax.bitcast_convert_type(h[:, w:].astype(BF16).astype(F32), U32)
    return (hi & jnp.uint32(0xFFFF0000)) | (lo >> 16)


def _unpack_rows(p):
    lo = lax.bitcast_convert_type(p << 16, F32)
    hi = lax.bitcast_convert_type(p & jnp.uint32(0xFFFF0000), F32)
    return lo, hi


def _ln_mod(v, lng, lnb, shift, scale):
    mu = jnp.mean(v, axis=-1, keepdims=True)
    d = v - mu
    var = jnp.mean(d * d, axis=-1, keepdims=True)
    xn = d * lax.rsqrt(var + LN_EPS) * lng + lnb
    return xn, xn * (1.0 + scale) + shift


def _adaln_kernel(c_ref, w1_ref, w2_ref, b_ref, o_ref):
    a = _silu(c_ref[...])
    t = jnp.dot(a, w1_ref[0], precision=HI, preferred_element_type=F32)
    o_ref[0] = jnp.dot(t, w2_ref[0], precision=HI, preferred_element_type=F32) + b_ref[0]


def _adaln(cond, w1, w2, b):
    depth, d, r = w1.shape
    n6 = w2.shape[2]
    tn = min(n6, 2048)
    rows = cond.shape[0]
    return pl.pallas_call(
        _adaln_kernel,
        grid=(depth, n6 // tn),
        in_specs=[
            pl.BlockSpec((rows, d), lambda l, j: (0, 0)),
            pl.BlockSpec((1, d, r), lambda l, j: (l, 0, 0)),
            pl.BlockSpec((1, r, tn), lambda l, j: (l, 0, j)),
            pl.BlockSpec((1, 1, tn), lambda l, j: (l, 0, j)),
        ],
        out_specs=pl.BlockSpec((1, rows, tn), lambda l, j: (l, 0, j)),
        out_shape=jax.ShapeDtypeStruct((depth, rows, n6), F32),
        compiler_params=_cp("arbitrary", "arbitrary"),
        name="adaln",
    )(cond, w1, w2, b.reshape(depth, 1, n6))


def _seg_fn(nt, nl):
    return lambda i: 2 * (i // nt) + (i % nt) // nl


def _modulate_kernel(x_ref, sh_ref, sc_ref, h_ref):
    h_ref[...] = (x_ref[...] * (1.0 + sc_ref[0]) + sh_ref[0]).astype(h_ref.dtype)


def _modulate(x, shift, scale, seg):
    n, d = x.shape
    row = pl.BlockSpec((TM, d), lambda i: (i, 0))
    vec = pl.BlockSpec((1, 1, d), lambda i: (seg(i), 0, 0))
    return pl.pallas_call(
        _modulate_kernel,
        grid=(n // TM,),
        in_specs=[row, vec, vec],
        out_specs=row,
        out_shape=jax.ShapeDtypeStruct((n, d), BF16),
        compiler_params=_cp("arbitrary"),
        name="modulate",
    )(x, shift, scale)


def _resid_ln_kernel(alpha, packed, x_ref, y_ref, g_ref, lng_ref, lnb_ref, sh_ref, sc_ref, xo_ref, h_ref):
    v = alpha * x_ref[...] + g_ref[0] * y_ref[...].astype(F32)
    xn, h = _ln_mod(v, lng_ref[0], lnb_ref[0], sh_ref[0], sc_ref[0])
    xo_ref[...] = xn
    h_ref[...] = _pack_rows(h) if packed else h.astype(h_ref.dtype)


def _resid_ln(x, y, gate, lng, lnb, shift, scale, seg, alpha, packed):
    n, d = x.shape
    row = pl.BlockSpec((TM, d), lambda i: (i, 0))
    vec = pl.BlockSpec((1, 1, d), lambda i: (seg(i), 0, 0))
    one = pl.BlockSpec((1, 1, d), lambda i: (0, 0, 0))
    if packed:
        h_spec, h_shape = pl.BlockSpec((TM, d // 2), lambda i: (i, 0)), jax.ShapeDtypeStruct((n, d // 2), U32)
    else:
        h_spec, h_shape = row, jax.ShapeDtypeStruct((n, d), BF16)
    return pl.pallas_call(
        functools.partial(_resid_ln_kernel, alpha, packed),
        grid=(n // TM,),
        in_specs=[row, row, vec, one, one, vec, vec],
        out_specs=[row, h_spec],
        out_shape=[jax.ShapeDtypeStruct((n, d), F32), h_shape],
        compiler_params=_cp("arbitrary"),
        name="resid_ln",
    )(x, y, gate, lng, lnb, shift, scale)


def _mm_kernel(x_ref, w_ref, o_ref, wbf_ref):
    @pl.when(pl.program_id(1) == 0)
    def _():
        wbf_ref[...] = w_ref[0].astype(BF16)

    o_ref[...] = _dot(x_ref[...], wbf_ref[...]).astype(o_ref.dtype)


TMM = 512


def _mm(x, w, layer, out_dtype, tn):
    n, k = x.shape
    m = w.shape[2]
    tn = min(tn, m)
    assert n % TMM == 0 and m % tn == 0
    return pl.pallas_call(
        _mm_kernel,
        grid=(m // tn, n // TMM),
        in_specs=[
            pl.BlockSpec((TMM, k), lambda j, i: (i, 0)),
            pl.BlockSpec((1, k, tn), lambda j, i: (layer, 0, j)),
        ],
        out_specs=pl.BlockSpec((TMM, tn), lambda j, i: (i, j)),
        out_shape=jax.ShapeDtypeStruct((n, m), out_dtype),
        scratch_shapes=[pltpu.VMEM((k, tn), BF16)],
        compiler_params=_cp("arbitrary", "arbitrary"),
        name="mm",
    )(x, w)


NA_HG = 2
NA_GROUP = 4
NA_UNION = NA_KH + NA_GROUP - 1


def _na_group_geometry(rows):
    cases, ids = [], []
    for g in range(rows // NA_GROUP):
        r0 = g * NA_GROUP
        u_start = int(np.clip(r0 - NA_KH // 2, 0, rows - NA_UNION))
        geo = []
        for u in range(NA_GROUP):
            r = r0 + u
            r_start = int(np.clip(r - NA_KH // 2, 0, rows - NA_KH))
            assert u_start <= r_start and r_start + NA_KH <= u_start + NA_UNION
            geo.append((r - u_start, r - r_start))
        geo = tuple(geo)
        if geo not in cases:
            cases.append(geo)
        ids.append(cases.index(geo))
    return cases, np.asarray(ids, np.int32)


def _na_bias_tables(rpb, cases):
    col = np.arange(GRID_W)
    c_start = np.clip(col - NA_KW // 2, 0, GRID_W - NA_KW)
    col_ok = (col[None, :] >= c_start[:, None]) & (col[None, :] < c_start[:, None] + NA_KW)
    dc = np.clip(col[None, :] - col[:, None], 1 - NA_KW, NA_KW - 1) + NA_KW - 1
    onehot = (dc[:, None, :] == np.arange(2 * NA_KW - 1)[None, :, None]).astype(np.float32)
    t15 = jnp.einsum("hrd,qdk->hrqk", rpb, jnp.asarray(onehot), precision=HI)
    t15 = jnp.where(col_ok[None, None], t15, NEG_INF)
    neg = jnp.full((rpb.shape[0], GRID_W, GRID_W), NEG_INF, F32)
    tabs = []
    for geo in cases:
        blocks = []
        for e, dl in geo:
            tiles = [t15[:, a - e + NA_KH - 1] if 0 <= a - (e - dl) < NA_KH else neg for a in range(NA_UNION)]
            blocks.append(jnp.concatenate(tiles, axis=-1))
        tabs.append(jnp.concatenate(blocks, axis=1))
    return jnp.stack(tabs, axis=1)


def _na_kernel(s_lat, c_ctx, hd, case_ref, q_ref, k_ref, v_ref, bias_ref, o_ref):
    rows = s_lat // GRID_W
    gq = NA_GROUP * GRID_W
    gk = NA_UNION * GRID_W
    scale = hd ** -0.5
    ctx = slice(s_lat, s_lat + c_ctx)

    def group(g, hh):
        ln = slice(hh * hd, (hh + 1) * hd)
        u_start = jnp.clip(g * NA_GROUP - NA_KH // 2, 0, rows - NA_UNION)
        q0 = pl.multiple_of(g * gq, gq)
        k0 = pl.multiple_of(u_start * GRID_W, GRID_W)
        q_g = q_ref[pl.ds(q0, gq), ln]
        s_band = _dot_nt(q_g, k_ref[pl.ds(k0, gk), ln]) * scale + bias_ref[hh, case_ref[g]]
        s_ctx = _dot_nt(q_g, k_ref[ctx, ln]) * scale
        m = jnp.maximum(jnp.max(s_band, axis=-1, keepdims=True), jnp.max(s_ctx, axis=-1, keepdims=True))
        p_band = jnp.exp(s_band - m)
        p_ctx = jnp.exp(s_ctx - m)
        den = jnp.sum(p_band, axis=-1, keepdims=True) + jnp.sum(p_ctx, axis=-1, keepdims=True)
        o = _dot(p_band.astype(BF16), v_ref[pl.ds(k0, gk), ln]) + _dot(p_ctx.astype(BF16), v_ref[ctx, ln])
        o_ref[pl.ds(q0, gq), ln] = (o / den).astype(o_ref.dtype)

    def body(g, carry):
        for hh in range(NA_HG):
            group(g, hh)
        return carry

    lax.fori_loop(0, rows // NA_GROUP, body, 0)
    for hh in range(NA_HG):
        ln = slice(hh * hd, (hh + 1) * hd)
        s = _dot_nt(q_ref[ctx, ln], k_ref[ctx, ln]) * scale
        p = jnp.exp(s - jnp.max(s, axis=-1, keepdims=True))
        o = _dot(p.astype(BF16), v_ref[ctx, ln]) / jnp.sum(p, axis=-1, keepdims=True)
        o_ref[ctx, ln] = o.astype(o_ref.dtype)


def _na_attention(qkv, rpb, batch, s_lat, c_ctx, d):
    r = s_lat + c_ctx
    hd = d // NA_HEADS
    wb = NA_HG * hd
    nhg = NA_HEADS // NA_HG
    rows = s_lat // GRID_W
    assert rows % NA_GROUP == 0 and rows >= NA_UNION
    cases, case_ids = _na_group_geometry(rows)
    bias = _na_bias_tables(rpb, cases)
    grid_spec = pltpu.PrefetchScalarGridSpec(
        num_scalar_prefetch=1,
        grid=(batch, nhg),
        in_specs=[
            pl.BlockSpec((r, wb), lambda b, g, c: (b, g)),
            pl.BlockSpec((r, wb), lambda b, g, c: (b, nhg + g)),
            pl.BlockSpec((r, wb), lambda b, g, c: (b, 2 * nhg + g)),
            pl.BlockSpec((NA_HG,) + bias.shape[1:], lambda b, g, c: (g, 0, 0, 0)),
        ],
        out_specs=pl.BlockSpec((r, wb), lambda b, g, c: (b, g)),
    )
    return pl.pallas_call(
        functools.partial(_na_kernel, s_lat, c_ctx, hd),
        grid_spec=grid_spec,
        out_shape=jax.ShapeDtypeStruct((batch * r, d), BF16),
        compiler_params=_cp("arbitrary", "arbitrary"),
        name="na_attention",
    )(jnp.asarray(case_ids), qkv, qkv, qkv, bias)


def _rope_tables(s_lat, c_ctx, dk):
    half = dk // 2
    nf = half // 2
    pos = np.arange(s_lat)
    inv = ROPE_BASE ** (-np.arange(nf, dtype=np.float32) / nf)
    ang_r = (pos // GRID_W).astype(np.float32)[:, None] * inv
    ang_c = (pos % GRID_W).astype(np.float32)[:, None] * inv
    cos = np.concatenate([np.cos(ang_r), np.cos(ang_r), np.cos(ang_c), np.cos(ang_c)], axis=1)
    sin = np.concatenate([-np.sin(ang_r), np.sin(ang_r), -np.sin(ang_c), np.sin(ang_c)], axis=1)
    cos = np.concatenate([cos, np.ones((c_ctx, dk), np.float32)], axis=0)
    sin = np.concatenate([sin, np.zeros((c_ctx, dk), np.float32)], axis=0)
    return jnp.asarray(cos, F32), jnp.asarray(sin, F32)


def _swap_quarters(u):
    parts = [pltpu.roll(u[:, i:i + LANES], LANES // 2, 1) for i in range(0, u.shape[1], LANES)]
    return jnp.concatenate(parts, axis=1)


def _split2(a):
    hi = a.astype(BF16)
    return hi, (a - hi.astype(F32)).astype(BF16)


def _gla_block(dk, dirn, q_ref, k_ref, v_ref, r_ref, cos_ref, sin_ref, w2_ref, ba_ref, o_ref, st_ref):
    L = GLA_CHUNK
    tb = q_ref.shape[0]
    nch = tb // L
    row = lax.broadcasted_iota(jnp.int32, (tb, tb), 0)
    col = lax.broadcasted_iota(jnp.int32, (tb, tb), 1)
    tri = ((row // L) == (col // L)) & ((row >= col) if dirn == 0 else (row <= col))
    cos = cos_ref[...]
    sin = sin_ref[...]
    q = q_ref[...].astype(F32) * (dk ** -0.5)
    k = k_ref[...].astype(F32)
    q = q * cos + _swap_quarters(q) * sin
    k = k * cos + _swap_quarters(k) * sin
    v = v_ref[...]
    r_hi, r_lo = _split2(r_ref[...])
    pre = _dot(jnp.concatenate([r_hi, r_lo, r_hi], axis=1), w2_ref[dirn, 0]) + ba_ref[dirn, 0]
    la = (jnp.minimum(pre, 0.0) - jnp.log(1.0 + jnp.exp(-jnp.abs(pre)))) / GLA_GATE_NORMALIZER
    la_hi = la.astype(BF16)
    la_mid, la_lo = _split2(la - la_hi.astype(F32))
    b3 = _dot(jnp.where(tri, 1.0, 0.0).astype(BF16), jnp.concatenate([la_hi, la_mid, la_lo], axis=1))
    b = b3[:, :dk] + b3[:, dk:2 * dk] + b3[:, 2 * dk:]
    qe, ke, qd, kd, decay = [], [], [], [], []
    for c in range(nch):
        sl = slice(c * L, (c + 1) * L)
        bc = b[sl]
        mc = bc[L // 2:L // 2 + 1]
        blc = bc[L - 1:L] if dirn == 0 else bc[0:1]
        qe.append((q[sl] * jnp.exp(bc - mc)).astype(BF16))
        ke.append((k[sl] * jnp.exp(mc - bc)).astype(BF16))
        qd.append((q[sl] * jnp.exp(bc)).astype(BF16))
        kd.append((k[sl] * jnp.exp(blc - bc)).astype(BF16))
        decay.append(jnp.exp(blc))
    attn = _dot_nt(jnp.concatenate(qe, axis=0), jnp.concatenate(ke, axis=0))
    intra = _dot(jnp.where(tri, attn, 0.0).astype(BF16), v)
    for c in (range(nch) if dirn == 0 else range(nch - 1, -1, -1)):
        sl = slice(c * L, (c + 1) * L)
        st = st_ref[dirn]
        o_ref[sl, :] = (intra[sl] + _dot_nt(qd[c], st.astype(BF16))).astype(o_ref.dtype)
        upd = lax.dot_general(v[sl], kd[c], (((0,), (0,)), ((), ())), preferred_element_type=F32)
        st_ref[dirn] = decay[c] * st + upd


def _gla_scan_kernel(dk, qf, kf, vf, rf, cosf, sinf, qb, kb, vb, rb, cosb, sinb, w2_ref, ba_ref, of_ref, ob_ref,
                     st_ref):
    @pl.when(pl.program_id(2) == 0)
    def _():
        st_ref[...] = jnp.zeros_like(st_ref)

    _gla_block(dk, 0, qf, kf, vf, rf, cosf, sinf, w2_ref, ba_ref, of_ref, st_ref)
    _gla_block(dk, 1, qb, kb, vb, rb, cosb, sinb, w2_ref, ba_ref, ob_ref, st_ref)


def _gla_scan(z, r, w2pad, ba, cos, sin, batch, s_lat, c_ctx, d):
    n = z.shape[0]
    hg = GLA_HEADS
    dkt = d // 2
    dk, dv = dkt // hg, d // hg
    nl = s_lat // TM
    nt = nl + 1
    assert c_ctx == TM

    blks = (lambda s: jnp.where(s == 0, nl, s - 1), lambda s: jnp.where(s == 0, nl, nl - s))
    in_specs = []
    for blk in blks:
        in_specs += [
            pl.BlockSpec((TM, dk), lambda b, h, s, blk=blk: (b * nt + blk(s), h)),
            pl.BlockSpec((TM, dk), lambda b, h, s, blk=blk: (b * nt + blk(s), hg + h)),
            pl.BlockSpec((TM, dv), lambda b, h, s, blk=blk: (b * nt + blk(s), 2 * dkt // dv + h)),
            pl.BlockSpec((TM, LANES), lambda b, h, s, blk=blk: (b * nt + blk(s), 0)),
            pl.BlockSpec((TM, dk), lambda b, h, s, blk=blk: (blk(s), 0)),
            pl.BlockSpec((TM, dk), lambda b, h, s, blk=blk: (blk(s), 0)),
        ]
    in_specs += [
        pl.BlockSpec((2, 1, 3 * LANES, dk), lambda b, h, s: (0, h, 0, 0)),
        pl.BlockSpec((2, 1, 1, dk), lambda b, h, s: (0, h, 0, 0)),
    ]
    out_specs = [pl.BlockSpec((TM, dv), lambda b, h, s, blk=blk: (b * nt + blk(s), h)) for blk in blks]
    return pl.pallas_call(
        functools.partial(_gla_scan_kernel, dk),
        grid=(batch, hg, nt),
        in_specs=in_specs,
        out_specs=out_specs,
        out_shape=[jax.ShapeDtypeStruct((n, d), F32), jax.ShapeDtypeStruct((n, d), F32)],
        scratch_shapes=[pltpu.VMEM((2, dv, dk), F32)],
        compiler_params=_cp("arbitrary", "arbitrary", "arbitrary"),
        name="gla_scan",
    )(z, z, z, r, cos, sin, z, z, z, r, cos, sin, w2pad, ba)


def _gla_out_kernel(dv, of_ref, ob_ref, g_ref, nw_ref, h_ref):
    nw = nw_ref[0]
    for h in range(of_ref.shape[1] // dv):
        sl = slice(h * dv, (h + 1) * dv)
        o = of_ref[:, sl] + ob_ref[:, sl]
        o = o * lax.rsqrt(jnp.mean(o * o, axis=-1, keepdims=True) + LN_EPS) * nw
        h_ref[:, sl] = (o * _silu(g_ref[:, sl].astype(F32))).astype(h_ref.dtype)


def _gla_out(o_f, o_b, z, norm_w, layer, d):
    n = z.shape[0]
    dv = d // GLA_HEADS
    gcol = (z.shape[1] - d) // d
    row = pl.BlockSpec((TM, d), lambda i: (i, 0))
    return pl.pallas_call(
        functools.partial(_gla_out_kernel, dv),
        grid=(n // TM,),
        in_specs=[
            row, row,
            pl.BlockSpec((TM, d), lambda i: (i, gcol)),
            pl.BlockSpec((1, 1, dv), lambda i: (layer, 0, 0)),
        ],
        out_specs=row,
        out_shape=jax.ShapeDtypeStruct((n, d), BF16),
        compiler_params=_cp("arbitrary"),
        name="gla_out",
    )(o_f, o_b, z, norm_w.reshape(norm_w.shape[0], 1, dv))


def _rank_desc(v, axis_len):
    idx = lax.broadcasted_iota(jnp.int32, v.shape, 0)
    rank = jnp.zeros(v.shape, jnp.int32)
    for j in range(axis_len):
        vj = v[j:j + 1]
        rank = rank + jnp.where((vj > v) | ((vj == v) & (j < idx)), 1, 0)
    return rank


def _router_kernel(n_e, x_ref, sh_ref, sc_ref, wr_ref, eb_ref, eid_ref, rnk_ref, gw_ref, cnt_ref, run_ref):
    @pl.when(pl.program_id(0) == 0)
    def _():
        run_ref[...] = jnp.zeros_like(run_ref)

    tm = x_ref.shape[0]
    gsz = n_e // N_GROUPS
    h = x_ref[...] * (1.0 + sc_ref[0]) + sh_ref[0]
    logits = lax.dot_general(wr_ref[0], h, (((1,), (1,)), ((), ())), precision=HI, preferred_element_type=F32)
    scores = _sigmoid(logits)
    biased = scores + eb_ref[0]
    grp = biased.reshape(N_GROUPS, gsz, tm)
    m1 = jnp.max(grp, axis=1, keepdims=True)
    gi = lax.broadcasted_iota(jnp.int32, grp.shape, 1)
    first = jnp.min(jnp.where(grp == m1, gi, gsz), axis=1, keepdims=True)
    m2 = jnp.max(jnp.where(gi == first, -jnp.inf, grp), axis=1, keepdims=True)
    g_sel = _rank_desc(m1 + m2, N_GROUPS) < TOPK_GROUPS
    e_sel = jnp.broadcast_to(g_sel, (N_GROUPS, gsz, tm)).reshape(n_e, tm)
    masked = jnp.where(e_sel, biased, NEG_INF)
    rank = _rank_desc(masked, n_e)
    sel = rank < TOP_K
    w = jnp.where(sel, scores, 0.0)
    gates = w / jnp.sum(w, axis=0, keepdims=True) * ROUTED_SCALE
    sel_f = jnp.where(sel, 1.0, 0.0)
    t0 = lax.broadcasted_iota(jnp.int32, (tm, tm), 0)
    t1 = lax.broadcasted_iota(jnp.int32, (tm, tm), 1)
    before = _dot(sel_f.astype(BF16), jnp.where(t0 < t1, 1.0, 0.0).astype(BF16))
    pos = run_ref[:, 0:1] + before
    run_ref[...] = run_ref[...] + jnp.sum(sel_f, axis=1, keepdims=True)
    cnt_ref[...] = run_ref[...].astype(jnp.int32)
    eidx = lax.broadcasted_iota(jnp.int32, (n_e, tm), 0).astype(F32)
    for j in range(TOP_K):
        hit = rank == j
        eid_ref[j:j + 1, :] = jnp.sum(jnp.where(hit, eidx, 0.0), axis=0, keepdims=True).astype(jnp.int32)
        rnk_ref[j:j + 1, :] = jnp.sum(jnp.where(hit, pos, 0.0), axis=0, keepdims=True).astype(jnp.int32)
        gw_ref[j:j + 1, :] = jnp.sum(jnp.where(hit, gates, 0.0), axis=0, keepdims=True)


def _router(x, shift, scale, w_router_t, e_bias, layer, seg):
    n, d = x.shape
    n_e = w_router_t.shape[1]
    vec = pl.BlockSpec((1, 1, d), lambda i: (seg(i), 0, 0))
    sel = pl.BlockSpec((TOP_K, TM), lambda i: (0, i))
    return pl.pallas_call(
        functools.partial(_router_kernel, n_e),
        grid=(n // TM,),
        in_specs=[
            pl.BlockSpec((TM, d), lambda i: (i, 0)),
            vec, vec,
            pl.BlockSpec((1, n_e, d), lambda i: (layer, 0, 0)),
            pl.BlockSpec((1, n_e, 1), lambda i: (layer, 0, 0)),
        ],
        out_specs=[sel, sel, sel, pl.BlockSpec((n_e, LANES), lambda i: (0, 0))],
        out_shape=[
            jax.ShapeDtypeStruct((TOP_K, n), jnp.int32),
            jax.ShapeDtypeStruct((TOP_K, n), jnp.int32),
            jax.ShapeDtypeStruct((TOP_K, n), F32),
            jax.ShapeDtypeStruct((n_e, LANES), jnp.int32),
        ],
        scratch_shapes=[pltpu.VMEM((n_e, LANES), F32)],
        compiler_params=_cp("arbitrary"),
        name="router",
    )(x, shift, scale, w_router_t, e_bias)


ROW_UNROLL = 8


def _gather_rows(idx_ref, src_ref, dst_ref, sem, n_rows):
    def group(g, c):
        for u in range(ROW_UNROLL):
            r = g * ROW_UNROLL + u
            pltpu.make_async_copy(src_ref.at[pl.ds(idx_ref[0, 0, r], 1)], dst_ref.at[pl.ds(r, 1)], sem).start()
        return c

    lax.fori_loop(0, n_rows // ROW_UNROLL, group, 0)


def _wait_rows(src_ref, dst_ref, sem, n_rows):
    pltpu.make_async_copy(src_ref.at[pl.ds(0, n_rows)], dst_ref, sem).wait()


def _expert_kernel(te_ref, nt_ref, tok_ref, tokn_ref, hp_ref, wg_ref, wu_ref, wd_ref, y_ref, xbuf, wgb, wub, wdb,
                   sem):
    j = pl.program_id(0)
    nt = nt_ref[0]
    tg = xbuf.shape[1]
    slot = lax.rem(j, 2)

    @pl.when((j == 0) & (nt > 0))
    def _():
        _gather_rows(tok_ref, hp_ref, xbuf.at[0], sem.at[0], tg)

    @pl.when(j + 1 < nt)
    def _():
        _gather_rows(tokn_ref, hp_ref, xbuf.at[1 - slot], sem.at[1 - slot], tg)

    @pl.when(j < nt)
    def _():
        @pl.when((j == 0) | (te_ref[j] != te_ref[jnp.maximum(j - 1, 0)]))
        def _():
            wgb[...] = wg_ref[0, 0].astype(BF16)
            wub[...] = wu_ref[0, 0].astype(BF16)
            wdb[...] = wd_ref[0, 0].astype(BF16)

        _wait_rows(hp_ref, xbuf.at[slot], sem.at[slot], tg)
        lo, hi = _unpack_rows(xbuf[slot])
        x = jnp.concatenate([lo.astype(BF16), hi.astype(BF16)], axis=1)
        hid = _silu(_dot(x, wgb[...])) * _dot(x, wub[...])
        y_ref[...] = _pack_rows(_dot(hid.astype(BF16), wdb[...]))

    @pl.when(j >= nt)
    def _():
        y_ref[...] = jnp.zeros_like(y_ref)


def _experts(hp, tok, tile_eid, n_tiles_used, w_gate, w_up, w_down, layer):
    n, dh = hp.shape
    d = 2 * dh
    f = w_gate.shape[3]
    n_tiles = tok.shape[0]
    grid_spec = pltpu.PrefetchScalarGridSpec(
        num_scalar_prefetch=2,
        grid=(n_tiles,),
        in_specs=[
            pl.BlockSpec((1, 1, TG), lambda j, te, nt: (j, 0, 0), memory_space=pltpu.SMEM),
            pl.BlockSpec((1, 1, TG), lambda j, te, nt: (jnp.minimum(j + 1, n_tiles - 1), 0, 0),
                         memory_space=pltpu.SMEM),
            pl.BlockSpec(memory_space=pl.ANY),
            pl.BlockSpec((1, 1, d, f), lambda j, te, nt: (layer, te[j], 0, 0)),
            pl.BlockSpec((1, 1, d, f), lambda j, te, nt: (layer, te[j], 0, 0)),
            pl.BlockSpec((1, 1, f, d), lambda j, te, nt: (layer, te[j], 0, 0)),
        ],
        out_specs=pl.BlockSpec((TG, dh), lambda j, te, nt: (j, 0)),
        scratch_shapes=[
            pltpu.VMEM((2, TG, dh), U32),
            pltpu.VMEM((d, f), BF16),
            pltpu.VMEM((d, f), BF16),
            pltpu.VMEM((f, d), BF16),
            pltpu.SemaphoreType.DMA((2,)),
        ],
    )
    return pl.pallas_call(
        _expert_kernel,
        grid_spec=grid_spec,
        out_shape=jax.ShapeDtypeStruct((n_tiles * TG, dh), U32),
        compiler_params=_cp("arbitrary"),
        name="experts",
    )(tile_eid, n_tiles_used, tok, tok, hp, w_gate, w_up, w_down)


def _shared_kernel(hp_ref, sg_ref, su_ref, sd_ref, y_ref, sgb, sub, sdb):
    @pl.when(pl.program_id(0) == 0)
    def _():
        sgb[...] = sg_ref[0].astype(BF16)
        sub[...] = su_ref[0].astype(BF16)
        sdb[...] = sd_ref[0].astype(BF16)

    lo, hi = _unpack_rows(hp_ref[...])
    x = jnp.concatenate([lo.astype(BF16), hi.astype(BF16)], axis=1)
    hid = _silu(_dot(x, sgb[...])) * _dot(x, sub[...])
    y_ref[...] = _dot(hid.astype(BF16), sdb[...]).astype(y_ref.dtype)


def _shared_expert(hp, sg, su, sd, layer):
    n, dh = hp.shape
    d = 2 * dh
    f = sg.shape[2]
    return pl.pallas_call(
        _shared_kernel,
        grid=(n // TM,),
        in_specs=[
            pl.BlockSpec((TM, dh), lambda i: (i, 0)),
            pl.BlockSpec((1, d, f), lambda i: (layer, 0, 0)),
            pl.BlockSpec((1, d, f), lambda i: (layer, 0, 0)),
            pl.BlockSpec((1, f, d), lambda i: (layer, 0, 0)),
        ],
        out_specs=pl.BlockSpec((TM, d), lambda i: (i, 0)),
        out_shape=jax.ShapeDtypeStruct((n, d), BF16),
        scratch_shapes=[pltpu.VMEM((d, f), BF16), pltpu.VMEM((d, f), BF16), pltpu.VMEM((f, d), BF16)],
        compiler_params=_cp("arbitrary"),
        name="shared_expert",
    )(hp, sg, su, sd)


TMC = 128


def _combine_kernel(alpha, dest_ref, destn_ref, ys_ref, ysh_ref, gw_ref, x_ref, g_ref, lng_ref, lnb_ref, sh_ref,
                    sc_ref, xo_ref, h_ref, buf, sem):
    tm, d = x_ref.shape
    dh = d // 2
    i = pl.program_id(0)
    slot = lax.rem(i, 2)
    n_rows = TOP_K * tm

    @pl.when(i == 0)
    def _():
        _gather_rows(dest_ref, ys_ref, buf.at[0], sem.at[0], n_rows)

    @pl.when(i + 1 < pl.num_programs(0))
    def _():
        _gather_rows(destn_ref, ys_ref, buf.at[1 - slot], sem.at[1 - slot], n_rows)

    _wait_rows(ys_ref, buf.at[slot], sem.at[slot], n_rows)
    ysh = ysh_ref[...].astype(F32)
    y_lo, y_hi = ysh[:, :dh], ysh[:, dh:]
    for kk in range(TOP_K):
        lo, hi = _unpack_rows(buf[slot, kk * tm:(kk + 1) * tm, :])
        w = gw_ref[:, kk:kk + 1]
        y_lo = y_lo + w * lo
        y_hi = y_hi + w * hi
    y = jnp.concatenate([y_lo, y_hi], axis=1)
    v = alpha * x_ref[...] + g_ref[0] * y
    xn, h = _ln_mod(v, lng_ref[0], lnb_ref[0], sh_ref[0], sc_ref[0])
    xo_ref[...] = xn
    h_ref[...] = h.astype(h_ref.dtype)


def _combine(dest, ys, ysh, gw, x, gate, lng, lnb, shift, scale, seg, alpha):
    n, d = x.shape
    row = pl.BlockSpec((TMC, d), lambda i: (i, 0))
    vec = pl.BlockSpec((1, 1, d), lambda i: (seg(i), 0, 0))
    one = pl.BlockSpec((1, 1, d), lambda i: (0, 0, 0))
    n_tiles = n // TMC
    return pl.pallas_call(
        functools.partial(_combine_kernel, alpha),
        grid=(n_tiles,),
        in_specs=[
            pl.BlockSpec((1, 1, TOP_K * TMC), lambda i: (i, 0, 0), memory_space=pltpu.SMEM),
            pl.BlockSpec((1, 1, TOP_K * TMC), lambda i: (jnp.minimum(i + 1, n_tiles - 1), 0, 0),
                         memory_space=pltpu.SMEM),
            pl.BlockSpec(memory_space=pl.ANY),
            row,
            pl.BlockSpec((TMC, TOP_K), lambda i: (i, 0)),
            row, vec, one, one, vec, vec,
        ],
        out_specs=[row, row],
        out_shape=[jax.ShapeDtypeStruct((n, d), F32), jax.ShapeDtypeStruct((n, d), BF16)],
        scratch_shapes=[pltpu.VMEM((2, TOP_K * TMC, d // 2), U32), pltpu.SemaphoreType.DMA((2,))],
        compiler_params=_cp("arbitrary"),
        name="moe_combine",
    )(dest, dest, ys, ysh, gw, x, gate, lng, lnb, shift, scale)


def _moe(x, hp, mods_f, w_router_t, e_bias, w_gate, w_up, w_down, sg, su, sd, gate, lng, lnb, nshift, nscale,
         layer, seg, seg_c, alpha):
    n, d = x.shape
    n_e = w_router_t.shape[1]
    shift_f, scale_f = mods_f
    eid, rnk, gw, cnt = _router(x, shift_f, scale_f, w_router_t, e_bias, layer, seg)
    counts = cnt[:, 0]
    tiles_e = (counts + TG - 1) // TG
    tile_end = jnp.cumsum(tiles_e)
    n_tiles = (n * TOP_K) // TG + n_e
    tile_ids = jnp.arange(n_tiles, dtype=jnp.int32)
    tile_eid = jnp.minimum(jnp.sum((tile_end[None, :] <= tile_ids[:, None]).astype(jnp.int32), axis=1), n_e - 1)
    row_off = (tile_end - tiles_e) * TG
    e_ids = jnp.arange(n_e, dtype=jnp.int32)
    dest = jnp.sum(jnp.where(eid[..., None] == e_ids, row_off, 0), axis=-1) + rnk
    tok = jnp.zeros((n_tiles * TG,), jnp.int32).at[dest.reshape(-1)].set(
        jnp.broadcast_to(jnp.arange(n, dtype=jnp.int32), (TOP_K, n)).reshape(-1))
    ys = _experts(hp, tok.reshape(n_tiles, 1, TG), tile_eid, tile_end[-1:].astype(jnp.int32), w_gate, w_up, w_down,
                  layer)
    ysh = _shared_expert(hp, sg, su, sd, layer)
    dest_t = dest.reshape(TOP_K, n // TMC, TMC).transpose(1, 0, 2).reshape(n // TMC, 1, TOP_K * TMC)
    return _combine(dest_t, ys, ysh, gw.T, x, gate, lng, lnb, nshift, nscale, seg_c, alpha)


def kernel(x, c, ctx, c_ctx, ada_w1, ada_w2, ada_b, ln_g, ln_b, na_w_qkv, na_w_o, na_rpb, gla_w_in, gla_w_a1,
           gla_w_a2, gla_b_a, gla_norm, gla_w_o, moe_router, moe_bias, moe_w_gate, moe_w_up, moe_w_down,
           sh_w_gate, sh_w_up, sh_w_down):
    batch, s_lat, d = x.shape
    c_ctx_len = ctx.shape[1]
    depth = ada_w1.shape[0]
    alpha = (2 * depth) ** 0.25
    assert s_lat % TM == 0 and c_ctx_len == TM and s_lat % GRID_W == 0
    r = s_lat + c_ctx_len
    n = batch * r
    nl, nt = s_lat // TM, r // TM
    seg = _seg_fn(nt, nl)
    seg_c = _seg_fn(nt * (TM // TMC), nl * (TM // TMC))

    xa = jnp.concatenate([x, ctx], axis=1).reshape(n, d)
    cond = jnp.stack([c, jnp.broadcast_to(c_ctx, c.shape)], axis=1).reshape(2 * batch, d)
    rows = -(-2 * batch // 8) * 8
    cond = jnp.pad(cond, ((0, rows - 2 * batch), (0, 0)))
    mods = _adaln(cond, ada_w1, ada_w2, ada_b).reshape(depth, rows, 6, d).transpose(0, 2, 1, 3)
    mods = mods.reshape(depth, 6, rows, 1, d)
    lng = ln_g.reshape(depth, 2, 1, 1, d)
    lnb = ln_b.reshape(depth, 2, 1, 1, d)

    hd_gla = d // 2 // GLA_HEADS
    cos, sin = _rope_tables(s_lat, c_ctx_len, hd_gla)
    n_e = moe_router.shape[2]
    w_router_t = jnp.swapaxes(moe_router, 1, 2)
    e_bias = moe_bias.reshape(depth, n_e, 1)
    rank_g = gla_w_a1.shape[3]
    assert 2 * rank_g <= LANES

    h = _modulate(xa, mods[0, 0], mods[0, 1], seg)
    for i in range(depth):
        last = i == depth - 1
        m = mods[i]
        j = i // 2
        if i % 2 == 0:
            qkv = _mm(h, na_w_qkv, j, BF16, 512)
            o = _na_attention(qkv, na_rpb[j], batch, s_lat, c_ctx_len, d)
            y = _mm(o, na_w_o, j, F32, 512)
        else:
            z = _mm(h, gla_w_in, j, BF16, 512)
            w_a1 = jnp.concatenate([gla_w_a1[j, 0], gla_w_a1[j, 1]], axis=1)
            w_a1 = jnp.pad(w_a1, ((0, 0), (0, LANES - 2 * rank_g)))[None]
            rr = _mm(h, w_a1, 0, F32, LANES)
            w2 = gla_w_a2[j].reshape(2, rank_g, GLA_HEADS, hd_gla).transpose(0, 2, 1, 3)
            w2pad = jnp.zeros((2, GLA_HEADS, LANES, hd_gla), F32)
            w2pad = w2pad.at[0, :, :rank_g].set(w2[0]).at[1, :, rank_g:2 * rank_g].set(w2[1])
            w2_hi = w2pad.astype(BF16)
            w2_lo = (w2pad - w2_hi.astype(F32)).astype(BF16)
            w2cat = jnp.concatenate([w2_hi, w2_hi, w2_lo], axis=2)
            ba = gla_b_a[j].reshape(2, GLA_HEADS, 1, hd_gla)
            o_f, o_b = _gla_scan(z, rr, w2cat, ba, cos, sin, batch, s_lat, c_ctx_len, d)
            og = _gla_out(o_f, o_b, z, gla_norm, j, d)
            y = _mm(og, gla_w_o, j, F32, 512)
        x1, hp = _resid_ln(xa, y, m[2], lng[i, 0], lnb[i, 0], m[3], m[4], seg, alpha, packed=True)
        nxt = mods[i + 1] if not last else m
        xa, h = _moe(x1, hp, (m[3], m[4]), w_router_t, e_bias, moe_w_gate, moe_w_up, moe_w_down,
                     sh_w_gate, sh_w_up, sh_w_down, m[5], lng[i, 1], lnb[i, 1], nxt[0], nxt[1],
                     i, seg, seg_c, alpha)
    return xa.reshape(batch, r, d)[:, :s_lat]
```

```python
import functools

import jax
import jax.numpy as jnp
import numpy as np
from jax import lax
from jax.experimental import pallas as pl
from jax.experimental.pallas import tpu as pltpu

GRID_W = 64
NA_HEADS = 32
NA_KH = 8
NA_KW = 16
GLA_HEADS = 8
GLA_GATE_NORMALIZER = 16.0
GLA_CHUNK = 64
ROPE_BASE = 10000.0
N_GROUPS = 8
TOPK_GROUPS = 4
TOP_K = 8
ROUTED_SCALE = 2.5
LN_EPS = 1e-6
NEG_INF = -1e30

LANES = 128
TM = 256
TG = 512
VMEM_LIMIT = 56 * 1024 * 1024

F32 = jnp.float32
BF16 = jnp.bfloat16
U32 = jnp.uint32
HI = lax.Precision.HIGHEST


def _cp(*sem):
    return pltpu.CompilerParams(dimension_semantics=sem, vmem_limit_bytes=VMEM_LIMIT)


def _sigmoid(x):
    return 1.0 / (1.0 + jnp.exp(-x))


def _silu(x):
    return x * _sigmoid(x)


def _dot(a, b):
    return jnp.dot(a, b, preferred_element_type=F32)


def _dot_nt(a, b):
    return lax.dot_general(a, b, (((1,), (1,)), ((), ())), preferred_element_type=F32)


def _pack_rows(h):
    w = h.shape[1] // 2
    lo = lax.bitcast_convert_type(h[:, :w].astype(BF16).astype(F32), U32)
    hi = lax.bitcast_convert_type(h[:, w:].astype(BF16).astype(F32), U32)
    return (hi & jnp.uint32(0xFFFF0000)) | (lo >> 16)


def _unpack_rows(p):
    lo = lax.bitcast_convert_type(p << 16, F32)
    hi = lax.bitcast_convert_type(p & jnp.uint32(0xFFFF0000), F32)
    return lo, hi


def _ln_mod(v, lng, lnb, shift, scale):
    mu = jnp.mean(v, axis=-1, keepdims=True)
    d = v - mu
    var = jnp.mean(d * d, axis=-1, keepdims=True)
    xn = d * lax.rsqrt(var + LN_EPS) * lng + lnb
    return xn, xn * (1.0 + scale) + shift


def _adaln_kernel(c_ref, w1_ref, w2_ref, b_ref, o_ref):
    a = _silu(c_ref[...])
    t = jnp.dot(a, w1_ref[0], precision=HI, preferred_element_type=F32)
    o_ref[0] = jnp.dot(t, w2_ref[0], precision=HI, preferred_element_type=F32) + b_ref[0]


def _adaln(cond, w1, w2, b):
    depth, d, r = w1.shape
    n6 = w2.shape[2]
    tn = min(n6, 2048)
    rows = cond.shape[0]
    return pl.pallas_call(
        _adaln_kernel,
        grid=(depth, n6 // tn),
        in_specs=[
            pl.BlockSpec((rows, d), lambda l, j: (0, 0)),
            pl.BlockSpec((1, d, r), lambda l, j: (l, 0, 0)),
            pl.BlockSpec((1, r, tn), lambda l, j: (l, 0, j)),
            pl.BlockSpec((1, 1, tn), lambda l, j: (l, 0, j)),
        ],
        out_specs=pl.BlockSpec((1, rows, tn), lambda l, j: (l, 0, j)),
        out_shape=jax.ShapeDtypeStruct((depth, rows, n6), F32),
        compiler_params=_cp("arbitrary", "arbitrary"),
        name="adaln",
    )(cond, w1, w2, b.reshape(depth, 1, n6))


def _seg_fn(nt, nl):
    return lambda i: 2 * (i // nt) + (i % nt) // nl


def _modulate_kernel(x_ref, sh_ref, sc_ref, h_ref):
    h_ref[...] = (x_ref[...] * (1.0 + sc_ref[0]) + sh_ref[0]).astype(h_ref.dtype)


def _modulate(x, shift, scale, seg):
    n, d = x.shape
    row = pl.BlockSpec((TM, d), lambda i: (i, 0))
    vec = pl.BlockSpec((1, 1, d), lambda i: (seg(i), 0, 0))
    return pl.pallas_call(
        _modulate_kernel,
        grid=(n // TM,),
        in_specs=[row, vec, vec],
        out_specs=row,
        out_shape=jax.ShapeDtypeStruct((n, d), BF16),
        compiler_params=_cp("arbitrary"),
        name="modulate",
    )(x, shift, scale)


def _resid_ln_kernel(alpha, packed, x_ref, y_ref, g_ref, lng_ref, lnb_ref, sh_ref, sc_ref, xo_ref, h_ref):
    v = alpha * x_ref[...] + g_ref[0] * y_ref[...].astype(F32)
    xn, h = _ln_mod(v, lng_ref[0], lnb_ref[0], sh_ref[0], sc_ref[0])
    xo_ref[...] = xn
    h_ref[...] = _pack_rows(h) if packed else h.astype(h_ref.dtype)


def _resid_ln(x, y, gate, lng, lnb, shift, scale, seg, alpha, packed):
    n, d = x.shape
    row = pl.BlockSpec((TM, d), lambda i: (i, 0))
    vec = pl.BlockSpec((1, 1, d), lambda i: (seg(i), 0, 0))
    one = pl.BlockSpec((1, 1, d), lambda i: (0, 0, 0))
    if packed:
        h_spec, h_shape = pl.BlockSpec((TM, d // 2), lambda i: (i, 0)), jax.ShapeDtypeStruct((n, d // 2), U32)
    else:
        h_spec, h_shape = row, jax.ShapeDtypeStruct((n, d), BF16)
    return pl.pallas_call(
        functools.partial(_resid_ln_kernel, alpha, packed),
        grid=(n // TM,),
        in_specs=[row, row, vec, one, one, vec, vec],
        out_specs=[row, h_spec],
        out_shape=[jax.ShapeDtypeStruct((n, d), F32), h_shape],
        compiler_params=_cp("arbitrary"),
        name="resid_ln",
    )(x, y, gate, lng, lnb, shift, scale)


def _mm_kernel(x_ref, w_ref, o_ref, wbf_ref):
    @pl.when(pl.program_id(1) == 0)
    def _():
        wbf_ref[...] = w_ref[0].astype(BF16)

    o_ref[...] = _dot(x_ref[...], wbf_ref[...]).astype(o_ref.dtype)


def _row_tile(n):
    for tm in (768, 512, 256):
        if n % tm == 0:
            return tm
    raise ValueError(f"no row tile divides {n}")


def _mm(x, w, layer, out_dtype, tn):
    n, k = x.shape
    m = w.shape[2]
    tn = min(tn, m)
    tm = _row_tile(n)
    assert m % tn == 0
    return pl.pallas_call(
        _mm_kernel,
        grid=(m // tn, n // tm),
        in_specs=[
            pl.BlockSpec((tm, k), lambda j, i: (i, 0)),
            pl.BlockSpec((1, k, tn), lambda j, i: (layer, 0, j)),
        ],
        out_specs=pl.BlockSpec((tm, tn), lambda j, i: (i, j)),
        out_shape=jax.ShapeDtypeStruct((n, m), out_dtype),
        scratch_shapes=[pltpu.VMEM((k, tn), BF16)],
        compiler_params=_cp("arbitrary", "arbitrary"),
        name="mm",
    )(x, w)


NA_HG = 2
NA_GROUP = 4
NA_UNION = NA_KH + NA_GROUP - 1


def _na_group_geometry(rows):
    cases, ids = [], []
    for g in range(rows // NA_GROUP):
        r0 = g * NA_GROUP
        u_start = int(np.clip(r0 - NA_KH // 2, 0, rows - NA_UNION))
        geo = []
        for u in range(NA_GROUP):
            r = r0 + u
            r_start = int(np.clip(r - NA_KH // 2, 0, rows - NA_KH))
            assert u_start <= r_start and r_start + NA_KH <= u_start + NA_UNION
            geo.append((r - u_start, r - r_start))
        geo = tuple(geo)
        if geo not in cases:
            cases.append(geo)
        ids.append(cases.index(geo))
    return cases, np.asarray(ids, np.int32)


def _na_bias_tables(rpb, cases):
    col = np.arange(GRID_W)
    c_start = np.clip(col - NA_KW // 2, 0, GRID_W - NA_KW)
    col_ok = (col[None, :] >= c_start[:, None]) & (col[None, :] < c_start[:, None] + NA_KW)
    dc = np.clip(col[None, :] - col[:, None], 1 - NA_KW, NA_KW - 1) + NA_KW - 1
    onehot = (dc[:, None, :] == np.arange(2 * NA_KW - 1)[None, :, None]).astype(np.float32)
    t15 = jnp.einsum("hrd,qdk->hrqk", rpb, jnp.asarray(onehot), precision=HI)
    t15 = jnp.where(col_ok[None, None], t15, NEG_INF)
    neg = jnp.full((rpb.shape[0], GRID_W, GRID_W), NEG_INF, F32)
    tabs = []
    for geo in cases:
        blocks = []
        for e, dl in geo:
            tiles = [t15[:, a - e + NA_KH - 1] if 0 <= a - (e - dl) < NA_KH else neg for a in range(NA_UNION)]
            blocks.append(jnp.concatenate(tiles, axis=-1))
        tabs.append(jnp.concatenate(blocks, axis=1))
    return jnp.stack(tabs, axis=1)


def _na_kernel(s_lat, c_ctx, hd, case_ref, q_ref, k_ref, v_ref, bias_ref, o_ref):
    rows = s_lat // GRID_W
    gq = NA_GROUP * GRID_W
    gk = NA_UNION * GRID_W
    scale = hd ** -0.5
    ctx = slice(s_lat, s_lat + c_ctx)

    def group(g, hh):
        ln = slice(hh * hd, (hh + 1) * hd)
        u_start = jnp.clip(g * NA_GROUP - NA_KH // 2, 0, rows - NA_UNION)
        q0 = pl.multiple_of(g * gq, gq)
        k0 = pl.multiple_of(u_start * GRID_W, GRID_W)
        q_g = q_ref[pl.ds(q0, gq), ln]
        s_band = _dot_nt(q_g, k_ref[pl.ds(k0, gk), ln]) * scale + bias_ref[hh, case_ref[g]]
        s_ctx = _dot_nt(q_g, k_ref[ctx, ln]) * scale
        m = jnp.maximum(jnp.max(s_band, axis=-1, keepdims=True), jnp.max(s_ctx, axis=-1, keepdims=True))
        p_band = jnp.exp(s_band - m)
        p_ctx = jnp.exp(s_ctx - m)
        den = jnp.sum(p_band, axis=-1, keepdims=True) + jnp.sum(p_ctx, axis=-1, keepdims=True)
        o = _dot(p_band.astype(BF16), v_ref[pl.ds(k0, gk), ln]) + _dot(p_ctx.astype(BF16), v_ref[ctx, ln])
        o_ref[pl.ds(q0, gq), ln] = (o / den).astype(o_ref.dtype)

    def body(g, carry):
        for hh in range(NA_HG):
            group(g, hh)
        return carry

    lax.fori_loop(0, rows // NA_GROUP, body, 0)
    for hh in range(NA_HG):
        ln = slice(hh * hd, (hh + 1) * hd)
        s = _dot_nt(q_ref[ctx, ln], k_ref[ctx, ln]) * scale
        p = jnp.exp(s - jnp.max(s, axis=-1, keepdims=True))
        o = _dot(p.astype(BF16), v_ref[ctx, ln]) / jnp.sum(p, axis=-1, keepdims=True)
        o_ref[ctx, ln] = o.astype(o_ref.dtype)


def _na_attention(qkv, rpb, batch, s_lat, c_ctx, d):
    r = s_lat + c_ctx
    hd = d // NA_HEADS
    wb = NA_HG * hd
    nhg = NA_HEADS // NA_HG
    rows = s_lat // GRID_W
    assert rows % NA_GROUP == 0 and rows >= NA_UNION
    cases, case_ids = _na_group_geometry(rows)
    bias = _na_bias_tables(rpb, cases)
    grid_spec = pltpu.PrefetchScalarGridSpec(
        num_scalar_prefetch=1,
        grid=(batch, nhg),
        in_specs=[
            pl.BlockSpec((r, wb), lambda b, g, c: (b, g)),
            pl.BlockSpec((r, wb), lambda b, g, c: (b, nhg + g)),
            pl.BlockSpec((r, wb), lambda b, g, c: (b, 2 * nhg + g)),
            pl.BlockSpec((NA_HG,) + bias.shape[1:], lambda b, g, c: (g, 0, 0, 0)),
        ],
        out_specs=pl.BlockSpec((r, wb), lambda b, g, c: (b, g)),
    )
    return pl.pallas_call(
        functools.partial(_na_kernel, s_lat, c_ctx, hd),
        grid_spec=grid_spec,
        out_shape=jax.ShapeDtypeStruct((batch * r, d), BF16),
        compiler_params=_cp("arbitrary", "arbitrary"),
        name="na_attention",
    )(jnp.asarray(case_ids), qkv, qkv, qkv, bias)


def _rope_tables(s_lat, c_ctx, dk):
    half = dk // 2
    nf = half // 2
    pos = np.arange(s_lat)
    inv = ROPE_BASE ** (-np.arange(nf, dtype=np.float32) / nf)
    ang_r = (pos // GRID_W).astype(np.float32)[:, None] * inv
    ang_c = (pos % GRID_W).astype(np.float32)[:, None] * inv
    cos = np.concatenate([np.cos(ang_r), np.cos(ang_r), np.cos(ang_c), np.cos(ang_c)], axis=1)
    sin = np.concatenate([-np.sin(ang_r), np.sin(ang_r), -np.sin(ang_c), np.sin(ang_c)], axis=1)
    cos = np.concatenate([cos, np.ones((c_ctx, dk), np.float32)], axis=0)
    sin = np.concatenate([sin, np.zeros((c_ctx, dk), np.float32)], axis=0)
    return jnp.asarray(cos, F32), jnp.asarray(sin, F32)


def _swap_quarters(u):
    parts = [pltpu.roll(u[:, i:i + LANES], LANES // 2, 1) for i in range(0, u.shape[1], LANES)]
    return jnp.concatenate(parts, axis=1)


def _split2(a):
    hi = a.astype(BF16)
    return hi, (a - hi.astype(F32)).astype(BF16)


def _gla_block(dk, dirn, q_ref, k_ref, v_ref, r_ref, cos_ref, sin_ref, w2_ref, ba_ref, o_ref, st_ref):
    L = GLA_CHUNK
    tb = q_ref.shape[0]
    nch = tb // L
    row = lax.broadcasted_iota(jnp.int32, (tb, tb), 0)
    col = lax.broadcasted_iota(jnp.int32, (tb, tb), 1)
    tri = ((row // L) == (col // L)) & ((row >= col) if dirn == 0 else (row <= col))
    cos = cos_ref[...]
    sin = sin_ref[...]
    q = q_ref[...].astype(F32) * (dk ** -0.5)
    k = k_ref[...].astype(F32)
    q = q * cos + _swap_quarters(q) * sin
    k = k * cos + _swap_quarters(k) * sin
    v = v_ref[...]
    r_hi, r_lo = _split2(r_ref[...])
    pre = _dot(jnp.concatenate([r_hi, r_lo, r_hi], axis=1), w2_ref[dirn, 0]) + ba_ref[dirn, 0]
    la = (jnp.minimum(pre, 0.0) - jnp.log(1.0 + jnp.exp(-jnp.abs(pre)))) / GLA_GATE_NORMALIZER
    la_hi = la.astype(BF16)
    la_mid, la_lo = _split2(la - la_hi.astype(F32))
    b3 = _dot(jnp.where(tri, 1.0, 0.0).astype(BF16), jnp.concatenate([la_hi, la_mid, la_lo], axis=1))
    b = b3[:, :dk] + b3[:, dk:2 * dk] + b3[:, 2 * dk:]
    qe, ke, qd, kd, decay = [], [], [], [], []
    for c in range(nch):
        sl = slice(c * L, (c + 1) * L)
        bc = b[sl]
        mc = bc[L // 2:L // 2 + 1]
        blc = bc[L - 1:L] if dirn == 0 else bc[0:1]
        qe.append((q[sl] * jnp.exp(bc - mc)).astype(BF16))
        ke.append((k[sl] * jnp.exp(mc - bc)).astype(BF16))
        qd.append((q[sl] * jnp.exp(bc)).astype(BF16))
        kd.append((k[sl] * jnp.exp(blc - bc)).astype(BF16))
        decay.append(jnp.exp(blc))
    attn = _dot_nt(jnp.concatenate(qe, axis=0), jnp.concatenate(ke, axis=0))
    intra = _dot(jnp.where(tri, attn, 0.0).astype(BF16), v)
    for c in (range(nch) if dirn == 0 else range(nch - 1, -1, -1)):
        sl = slice(c * L, (c + 1) * L)
        st = st_ref[dirn]
        o_ref[sl, :] = (intra[sl] + _dot_nt(qd[c], st.astype(BF16))).astype(o_ref.dtype)
        upd = lax.dot_general(v[sl], kd[c], (((0,), (0,)), ((), ())), preferred_element_type=F32)
        st_ref[dirn] = decay[c] * st + upd


def _gla_scan_kernel(dk, qf, kf, vf, rf, cosf, sinf, qb, kb, vb, rb, cosb, sinb, w2_ref, ba_ref, of_ref, ob_ref,
                     st_ref):
    @pl.when(pl.program_id(2) == 0)
    def _():
        st_ref[...] = jnp.zeros_like(st_ref)

    _gla_block(dk, 0, qf, kf, vf, rf, cosf, sinf, w2_ref, ba_ref, of_ref, st_ref)
    _gla_block(dk, 1, qb, kb, vb, rb, cosb, sinb, w2_ref, ba_ref, ob_ref, st_ref)


def _gla_scan(z, r, w2pad, ba, cos, sin, batch, s_lat, c_ctx, d):
    n = z.shape[0]
    hg = GLA_HEADS
    dkt = d // 2
    dk, dv = dkt // hg, d // hg
    nl = s_lat // TM
    nt = nl + 1
    assert c_ctx == TM

    blks = (lambda s: jnp.where(s == 0, nl, s - 1), lambda s: jnp.where(s == 0, nl, nl - s))
    in_specs = []
    for blk in blks:
        in_specs += [
            pl.BlockSpec((TM, dk), lambda b, h, s, blk=blk: (b * nt + blk(s), h)),
            pl.BlockSpec((TM, dk), lambda b, h, s, blk=blk: (b * nt + blk(s), hg + h)),
            pl.BlockSpec((TM, dv), lambda b, h, s, blk=blk: (b * nt + blk(s), 2 * dkt // dv + h)),
            pl.BlockSpec((TM, LANES), lambda b, h, s, blk=blk: (b * nt + blk(s), 0)),
            pl.BlockSpec((TM, dk), lambda b, h, s, blk=blk: (blk(s), 0)),
            pl.BlockSpec((TM, dk), lambda b, h, s, blk=blk: (blk(s), 0)),
        ]
    in_specs += [
        pl.BlockSpec((2, 1, 3 * LANES, dk), lambda b, h, s: (0, h, 0, 0)),
        pl.BlockSpec((2, 1, 1, dk), lambda b, h, s: (0, h, 0, 0)),
    ]
    out_specs = [pl.BlockSpec((TM, dv), lambda b, h, s, blk=blk: (b * nt + blk(s), h)) for blk in blks]
    return pl.pallas_call(
        functools.partial(_gla_scan_kernel, dk),
        grid=(batch, hg, nt),
        in_specs=in_specs,
        out_specs=out_specs,
        out_shape=[jax.ShapeDtypeStruct((n, d), F32), jax.ShapeDtypeStruct((n, d), F32)],
        scratch_shapes=[pltpu.VMEM((2, dv, dk), F32)],
        compiler_params=_cp("arbitrary", "arbitrary", "arbitrary"),
        name="gla_scan",
    )(z, z, z, r, cos, sin, z, z, z, r, cos, sin, w2pad, ba)


def _gla_out_kernel(dv, of_ref, ob_ref, g_ref, nw_ref, h_ref):
    nw = nw_ref[0]
    for h in range(of_ref.shape[1] // dv):
        sl = slice(h * dv, (h + 1) * dv)
        o = of_ref[:, sl] + ob_ref[:, sl]
        o = o * lax.rsqrt(jnp.mean(o * o, axis=-1, keepdims=True) + LN_EPS) * nw
        h_ref[:, sl] = (o * _silu(g_ref[:, sl].astype(F32))).astype(h_ref.dtype)


def _gla_out(o_f, o_b, z, norm_w, layer, d):
    n = z.shape[0]
    dv = d // GLA_HEADS
    gcol = (z.shape[1] - d) // d
    row = pl.BlockSpec((TM, d), lambda i: (i, 0))
    return pl.pallas_call(
        functools.partial(_gla_out_kernel, dv),
        grid=(n // TM,),
        in_specs=[
            row, row,
            pl.BlockSpec((TM, d), lambda i: (i, gcol)),
            pl.BlockSpec((1, 1, dv), lambda i: (layer, 0, 0)),
        ],
        out_specs=row,
        out_shape=jax.ShapeDtypeStruct((n, d), BF16),
        compiler_params=_cp("arbitrary"),
        name="gla_out",
    )(o_f, o_b, z, norm_w.reshape(norm_w.shape[0], 1, dv))


def _rank_desc(v, axis_len):
    idx = lax.broadcasted_iota(jnp.int32, v.shape, 0)
    rank = jnp.zeros(v.shape, jnp.int32)
    for j in range(axis_len):
        vj = v[j:j + 1]
        rank = rank + jnp.where((vj > v) | ((vj == v) & (j < idx)), 1, 0)
    return rank


def _router_kernel(n_e, x_ref, sh_ref, sc_ref, wr_ref, eb_ref, eid_ref, rnk_ref, gw_ref, cnt_ref, run_ref):
    @pl.when(pl.program_id(0) == 0)
    def _():
        run_ref[...] = jnp.zeros_like(run_ref)

    tm = x_ref.shape[0]
    gsz = n_e // N_GROUPS
    h = x_ref[...] * (1.0 + sc_ref[0]) + sh_ref[0]
    logits = lax.dot_general(wr_ref[0], h, (((1,), (1,)), ((), ())), precision=HI, preferred_element_type=F32)
    scores = _sigmoid(logits)
    biased = scores + eb_ref[0]
    grp = biased.reshape(N_GROUPS, gsz, tm)
    m1 = jnp.max(grp, axis=1, keepdims=True)
    gi = lax.broadcasted_iota(jnp.int32, grp.shape, 1)
    first = jnp.min(jnp.where(grp == m1, gi, gsz), axis=1, keepdims=True)
    m2 = jnp.max(jnp.where(gi == first, -jnp.inf, grp), axis=1, keepdims=True)
    g_sel = _rank_desc(m1 + m2, N_GROUPS) < TOPK_GROUPS
    e_sel = jnp.broadcast_to(g_sel, (N_GROUPS, gsz, tm)).reshape(n_e, tm)
    masked = jnp.where(e_sel, biased, NEG_INF)
    rank = _rank_desc(masked, n_e)
    sel = rank < TOP_K
    w = jnp.where(sel, scores, 0.0)
    gates = w / jnp.sum(w, axis=0, keepdims=True) * ROUTED_SCALE
    sel_f = jnp.where(sel, 1.0, 0.0)
    t0 = lax.broadcasted_iota(jnp.int32, (tm, tm), 0)
    t1 = lax.broadcasted_iota(jnp.int32, (tm, tm), 1)
    before = _dot(sel_f.astype(BF16), jnp.where(t0 < t1, 1.0, 0.0).astype(BF16))
    pos = run_ref[:, 0:1] + before
    run_ref[...] = run_ref[...] + jnp.sum(sel_f, axis=1, keepdims=True)
    cnt_ref[...] = run_ref[...].astype(jnp.int32)
    eidx = lax.broadcasted_iota(jnp.int32, (n_e, tm), 0).astype(F32)
    for j in range(TOP_K):
        hit = rank == j
        eid_ref[j:j + 1, :] = jnp.sum(jnp.where(hit, eidx, 0.0), axis=0, keepdims=True).astype(jnp.int32)
        rnk_ref[j:j + 1, :] = jnp.sum(jnp.where(hit, pos, 0.0), axis=0, keepdims=True).astype(jnp.int32)
        gw_ref[j:j + 1, :] = jnp.sum(jnp.where(hit, gates, 0.0), axis=0, keepdims=True)


def _router(x, shift, scale, w_router_t, e_bias, layer, seg):
    n, d = x.shape
    n_e = w_router_t.shape[1]
    vec = pl.BlockSpec((1, 1, d), lambda i: (seg(i), 0, 0))
    sel = pl.BlockSpec((TOP_K, TM), lambda i: (0, i))
    return pl.pallas_call(
        functools.partial(_router_kernel, n_e),
        grid=(n // TM,),
        in_specs=[
            pl.BlockSpec((TM, d), lambda i: (i, 0)),
            vec, vec,
            pl.BlockSpec((1, n_e, d), lambda i: (layer, 0, 0)),
            pl.BlockSpec((1, n_e, 1), lambda i: (layer, 0, 0)),
        ],
        out_specs=[sel, sel, sel, pl.BlockSpec((n_e, LANES), lambda i: (0, 0))],
        out_shape=[
            jax.ShapeDtypeStruct((TOP_K, n), jnp.int32),
            jax.ShapeDtypeStruct((TOP_K, n), jnp.int32),
            jax.ShapeDtypeStruct((TOP_K, n), F32),
            jax.ShapeDtypeStruct((n_e, LANES), jnp.int32),
        ],
        scratch_shapes=[pltpu.VMEM((n_e, LANES), F32)],
        compiler_params=_cp("arbitrary"),
        name="router",
    )(x, shift, scale, w_router_t, e_bias)


ROW_UNROLL = 8


def _gather_rows(idx_ref, src_ref, dst_ref, sem, n_rows):
    def group(g, c):
        for u in range(ROW_UNROLL):
            t = idx_ref[0, 0, g * ROW_UNROLL + u]
            src = src_ref.at[lax.shift_right_logical(t, 3), pl.ds(t & 7, 1)]
            pltpu.make_async_copy(src, dst_ref.at[g, pl.ds(u, 1)], sem).start(priority=u % 2)
        return c

    lax.fori_loop(0, n_rows // ROW_UNROLL, group, 0)


def _wait_rows(src_ref, dst_ref, sem, n_rows):
    pltpu.make_async_copy(src_ref.at[pl.ds(0, n_rows // ROW_UNROLL)], dst_ref, sem).wait()


def _expert_kernel(te_ref, nt_ref, tok_ref, tokn_ref, hp_ref, wg_ref, wu_ref, wd_ref, y_ref, xbuf, wgb, wub, wdb,
                   sem):
    j = pl.program_id(0)
    nt = nt_ref[0]
    tg = xbuf.shape[1] * xbuf.shape[2]
    slot = lax.rem(j, 2)

    @pl.when((j == 0) & (nt > 0))
    def _():
        _gather_rows(tok_ref, hp_ref, xbuf.at[0], sem.at[0], tg)

    @pl.when(j + 1 < nt)
    def _():
        _gather_rows(tokn_ref, hp_ref, xbuf.at[1 - slot], sem.at[1 - slot], tg)

    @pl.when(j < nt)
    def _():
        @pl.when((j == 0) | (te_ref[j] != te_ref[jnp.maximum(j - 1, 0)]))
        def _():
            wgb[...] = wg_ref[0, 0].astype(BF16)
            wub[...] = wu_ref[0, 0].astype(BF16)
            wdb[...] = wd_ref[0, 0].astype(BF16)

        _wait_rows(hp_ref, xbuf.at[slot], sem.at[slot], tg)
        lo, hi = _unpack_rows(xbuf[slot].reshape(tg, xbuf.shape[3]))
        x = jnp.concatenate([lo.astype(BF16), hi.astype(BF16)], axis=1)
        hid = _silu(_dot_nt(x, wgb[...])) * _dot_nt(x, wub[...])
        y_ref[...] = _pack_rows(_dot(hid.astype(BF16), wdb[...]))

    @pl.when(j >= nt)
    def _():
        y_ref[...] = jnp.zeros_like(y_ref)


def _experts(hp, tok, tile_eid, n_tiles_used, w_gate_t, w_up_t, w_down, layer):
    n, dh = hp.shape
    d = 2 * dh
    f = w_gate_t.shape[2]
    n_tiles = tok.shape[0]
    assert n % ROW_UNROLL == 0 and TG % ROW_UNROLL == 0
    grid_spec = pltpu.PrefetchScalarGridSpec(
        num_scalar_prefetch=2,
        grid=(n_tiles,),
        in_specs=[
            pl.BlockSpec((1, 1, TG), lambda j, te, nt: (j, 0, 0), memory_space=pltpu.SMEM),
            pl.BlockSpec((1, 1, TG), lambda j, te, nt: (jnp.minimum(j + 1, n_tiles - 1), 0, 0),
                         memory_space=pltpu.SMEM),
            pl.BlockSpec(memory_space=pl.ANY),
            pl.BlockSpec((1, 1, f, d), lambda j, te, nt: (layer, te[j], 0, 0)),
            pl.BlockSpec((1, 1, f, d), lambda j, te, nt: (layer, te[j], 0, 0)),
            pl.BlockSpec((1, 1, f, d), lambda j, te, nt: (layer, te[j], 0, 0)),
        ],
        out_specs=pl.BlockSpec((TG, dh), lambda j, te, nt: (j, 0)),
        scratch_shapes=[
            pltpu.VMEM((2, TG // ROW_UNROLL, ROW_UNROLL, dh), U32),
            pltpu.VMEM((f, d), BF16),
            pltpu.VMEM((f, d), BF16),
            pltpu.VMEM((f, d), BF16),
            pltpu.SemaphoreType.DMA((2,)),
        ],
    )
    return pl.pallas_call(
        _expert_kernel,
        grid_spec=grid_spec,
        out_shape=jax.ShapeDtypeStruct((n_tiles * TG, dh), U32),
        compiler_params=_cp("arbitrary"),
        name="experts",
    )(tile_eid, n_tiles_used, tok, tok, hp.reshape(n // ROW_UNROLL, ROW_UNROLL, dh), w_gate_t, w_up_t, w_down)


def _shared_kernel(hp_ref, sg_ref, su_ref, sd_ref, y_ref, sgb, sub, sdb):
    @pl.when(pl.program_id(0) == 0)
    def _():
        sgb[...] = sg_ref[0].astype(BF16)
        sub[...] = su_ref[0].astype(BF16)
        sdb[...] = sd_ref[0].astype(BF16)

    lo, hi = _unpack_rows(hp_ref[...])
    x = jnp.concatenate([lo.astype(BF16), hi.astype(BF16)], axis=1)
    hid = _silu(_dot_nt(x, sgb[...])) * _dot_nt(x, sub[...])
    y_ref[...] = _dot(hid.astype(BF16), sdb[...]).astype(y_ref.dtype)


def _shared_expert(hp, sg_t, su_t, sd, layer):
    n, dh = hp.shape
    d = 2 * dh
    f = sg_t.shape[1]
    tm = _row_tile(n)
    wspec = pl.BlockSpec((1, f, d), lambda i: (layer, 0, 0))
    return pl.pallas_call(
        _shared_kernel,
        grid=(n // tm,),
        in_specs=[pl.BlockSpec((tm, dh), lambda i: (i, 0)), wspec, wspec, wspec],
        out_specs=pl.BlockSpec((tm, d), lambda i: (i, 0)),
        out_shape=jax.ShapeDtypeStruct((n, d), BF16),
        scratch_shapes=[pltpu.VMEM((f, d), BF16), pltpu.VMEM((f, d), BF16), pltpu.VMEM((f, d), BF16)],
        compiler_params=_cp("arbitrary"),
        name="shared_expert",
    )(hp, sg_t, su_t, sd)


TMC = 128


def _combine_kernel(alpha, dest_ref, destn_ref, ys_ref, ysh_ref, gw_ref, x_ref, g_ref, lng_ref, lnb_ref, sh_ref,
                    sc_ref, xo_ref, h_ref, buf, sem):
    tm, d = x_ref.shape
    dh = d // 2
    i = pl.program_id(0)
    slot = lax.rem(i, 2)
    n_rows = TOP_K * tm

    @pl.when(i == 0)
    def _():
        _gather_rows(dest_ref, ys_ref, buf.at[0], sem.at[0], n_rows)

    @pl.when(i + 1 < pl.num_programs(0))
    def _():
        _gather_rows(destn_ref, ys_ref, buf.at[1 - slot], sem.at[1 - slot], n_rows)

    _wait_rows(ys_ref, buf.at[slot], sem.at[slot], n_rows)
    ysh = ysh_ref[...].astype(F32)
    y_lo, y_hi = ysh[:, :dh], ysh[:, dh:]
    tg = tm // ROW_UNROLL
    for kk in range(TOP_K):
        lo, hi = _unpack_rows(buf[slot, kk * tg:(kk + 1) * tg].reshape(tm, dh))
        w = gw_ref[:, kk:kk + 1]
        y_lo = y_lo + w * lo
        y_hi = y_hi + w * hi
    y = jnp.concatenate([y_lo, y_hi], axis=1)
    v = alpha * x_ref[...] + g_ref[0] * y
    xn, h = _ln_mod(v, lng_ref[0], lnb_ref[0], sh_ref[0], sc_ref[0])
    xo_ref[...] = xn
    h_ref[...] = h.astype(h_ref.dtype)


def _combine(dest, ys, ysh, gw, x, gate, lng, lnb, shift, scale, seg, alpha):
    n, d = x.shape
    row = pl.BlockSpec((TMC, d), lambda i: (i, 0))
    vec = pl.BlockSpec((1, 1, d), lambda i: (seg(i), 0, 0))
    one = pl.BlockSpec((1, 1, d), lambda i: (0, 0, 0))
    n_tiles = n // TMC
    return pl.pallas_call(
        functools.partial(_combine_kernel, alpha),
        grid=(n_tiles,),
        in_specs=[
            pl.BlockSpec((1, 1, TOP_K * TMC), lambda i: (i, 0, 0), memory_space=pltpu.SMEM),
            pl.BlockSpec((1, 1, TOP_K * TMC), lambda i: (jnp.minimum(i + 1, n_tiles - 1), 0, 0),
                         memory_space=pltpu.SMEM),
            pl.BlockSpec(memory_space=pl.ANY),
            row,
            pl.BlockSpec((TMC, TOP_K), lambda i: (i, 0)),
            row, vec, one, one, vec, vec,
        ],
        out_specs=[row, row],
        out_shape=[jax.ShapeDtypeStruct((n, d), F32), jax.ShapeDtypeStruct((n, d), BF16)],
        scratch_shapes=[pltpu.VMEM((2, TOP_K * TMC // ROW_UNROLL, ROW_UNROLL, d // 2), U32),
                        pltpu.SemaphoreType.DMA((2,))],
        compiler_params=_cp("arbitrary"),
        name="moe_combine",
    )(dest, dest, ys.reshape(ys.shape[0] // ROW_UNROLL, ROW_UNROLL, d // 2), ysh, gw, x, gate, lng, lnb, shift, scale)


def _moe(x, hp, mods_f, w_router_t, e_bias, w_gate, w_up, w_down, sg, su, sd, gate, lng, lnb, nshift, nscale,
         layer, seg, seg_c, alpha):
    n, d = x.shape
    n_e = w_router_t.shape[1]
    shift_f, scale_f = mods_f
    eid, rnk, gw, cnt = _router(x, shift_f, scale_f, w_router_t, e_bias, layer, seg)
    counts = cnt[:, 0]
    tiles_e = (counts + TG - 1) // TG
    tile_end = jnp.cumsum(tiles_e)
    n_tiles = (n * TOP_K) // TG + n_e
    tile_ids = jnp.arange(n_tiles, dtype=jnp.int32)
    tile_eid = jnp.minimum(jnp.sum((tile_end[None, :] <= tile_ids[:, None]).astype(jnp.int32), axis=1), n_e - 1)
    row_off = (tile_end - tiles_e) * TG
    e_ids = jnp.arange(n_e, dtype=jnp.int32)
    dest = jnp.sum(jnp.where(eid[..., None] == e_ids, row_off, 0), axis=-1) + rnk
    tok = jnp.zeros((n_tiles * TG,), jnp.int32).at[dest.reshape(-1)].set(
        jnp.broadcast_to(jnp.arange(n, dtype=jnp.int32), (TOP_K, n)).reshape(-1),
        unique_indices=True, mode="promise_in_bounds")
    ys = _experts(hp, tok.reshape(n_tiles, 1, TG), tile_eid, tile_end[-1:].astype(jnp.int32), w_gate, w_up, w_down,
                  layer)
    ysh = _shared_expert(hp, sg, su, sd, layer)
    dest_t = dest.reshape(TOP_K, n // TMC, TMC).transpose(1, 0, 2).reshape(n // TMC, 1, TOP_K * TMC)
    return _combine(dest_t, ys, ysh, gw.T, x, gate, lng, lnb, nshift, nscale, seg_c, alpha)


def kernel(x, c, ctx, c_ctx, ada_w1, ada_w2, ada_b, ln_g, ln_b, na_w_qkv, na_w_o, na_rpb, gla_w_in, gla_w_a1,
           gla_w_a2, gla_b_a, gla_norm, gla_w_o, moe_router, moe_bias, moe_w_gate, moe_w_up, moe_w_down,
           sh_w_gate, sh_w_up, sh_w_down):
    batch, s_lat, d = x.shape
    c_ctx_len = ctx.shape[1]
    depth = ada_w1.shape[0]
    alpha = (2 * depth) ** 0.25
    assert s_lat % TM == 0 and c_ctx_len == TM and s_lat % GRID_W == 0
    r = s_lat + c_ctx_len
    n = batch * r
    nl, nt = s_lat // TM, r // TM
    seg = _seg_fn(nt, nl)
    seg_c = _seg_fn(nt * (TM // TMC), nl * (TM // TMC))

    xa = jnp.concatenate([x, ctx], axis=1).reshape(n, d)
    cond = jnp.stack([c, jnp.broadcast_to(c_ctx, c.shape)], axis=1).reshape(2 * batch, d)
    rows = -(-2 * batch // 8) * 8
    cond = jnp.pad(cond, ((0, rows - 2 * batch), (0, 0)))
    mods = _adaln(cond, ada_w1, ada_w2, ada_b).reshape(depth, rows, 6, d).transpose(0, 2, 1, 3)
    mods = mods.reshape(depth, 6, rows, 1, d)
    lng = ln_g.reshape(depth, 2, 1, 1, d)
    lnb = ln_b.reshape(depth, 2, 1, 1, d)

    hd_gla = d // 2 // GLA_HEADS
    cos, sin = _rope_tables(s_lat, c_ctx_len, hd_gla)
    n_e = moe_router.shape[2]
    w_router_t = jnp.swapaxes(moe_router, 1, 2)
    e_bias = moe_bias.reshape(depth, n_e, 1)
    w_gate_t, w_up_t = jnp.swapaxes(moe_w_gate, 2, 3), jnp.swapaxes(moe_w_up, 2, 3)
    sg_t, su_t = jnp.swapaxes(sh_w_gate, 1, 2), jnp.swapaxes(sh_w_up, 1, 2)
    rank_g = gla_w_a1.shape[3]
    assert 2 * rank_g <= LANES

    h = _modulate(xa, mods[0, 0], mods[0, 1], seg)
    for i in range(depth):
        last = i == depth - 1
        m = mods[i]
        j = i // 2
        if i % 2 == 0:
            qkv = _mm(h, na_w_qkv, j, BF16, 512)
            o = _na_attention(qkv, na_rpb[j], batch, s_lat, c_ctx_len, d)
            y = _mm(o, na_w_o, j, F32, 512)
        else:
            z = _mm(h, gla_w_in, j, BF16, 512)
            w_a1 = jnp.concatenate([gla_w_a1[j, 0], gla_w_a1[j, 1]], axis=1)
            w_a1 = jnp.pad(w_a1, ((0, 0), (0, LANES - 2 * rank_g)))[None]
            rr = _mm(h, w_a1, 0, F32, LANES)
            w2 = gla_w_a2[j].reshape(2, rank_g, GLA_HEADS, hd_gla).transpose(0, 2, 1, 3)
            w2pad = jnp.zeros((2, GLA_HEADS, LANES, hd_gla), F32)
            w2pad = w2pad.at[0, :, :rank_g].set(w2[0]).at[1, :, rank_g:2 * rank_g].set(w2[1])
            w2_hi = w2pad.astype(BF16)
            w2_lo = (w2pad - w2_hi.astype(F32)).astype(BF16)
            w2cat = jnp.concatenate([w2_hi, w2_hi, w2_lo], axis=2)
            ba = gla_b_a[j].reshape(2, GLA_HEADS, 1, hd_gla)
            o_f, o_b = _gla_scan(z, rr, w2cat, ba, cos, sin, batch, s_lat, c_ctx_len, d)
            og = _gla_out(o_f, o_b, z, gla_norm, j, d)
            y = _mm(og, gla_w_o, j, F32, 512)
        x1, hp = _resid_ln(xa, y, m[2], lng[i, 0], lnb[i, 0], m[3], m[4], seg, alpha, packed=True)
        nxt = mods[i + 1] if not last else m
        xa, h = _moe(x1, hp, (m[3], m[4]), w_router_t, e_bias, w_gate_t, w_up_t, moe_w_down,
                     sg_t, su_t, sh_w_down, m[5], lng[i, 1], lnb[i, 1], nxt[0], nxt[1],
                     i, seg, seg_c, alpha)
    return xa.reshape(batch, r, d)[:, :s_lat]
```

```python
import functools

import jax
import jax.numpy as jnp
import numpy as np
from jax import lax
from jax.experimental import pallas as pl
from jax.experimental.pallas import tpu as pltpu

GRID_W = 64
NA_HEADS = 32
NA_KH = 8
NA_KW = 16
GLA_HEADS = 8
GLA_GATE_NORMALIZER = 16.0
GLA_CHUNK = 64
ROPE_BASE = 10000.0
N_GROUPS = 8
TOPK_GROUPS = 4
TOP_K = 8
ROUTED_SCALE = 2.5
LN_EPS = 1e-6
NEG_INF = -1e30

LANES = 128
TM = 256
TG = 512
VMEM_LIMIT = 56 * 1024 * 1024

F32 = jnp.float32
BF16 = jnp.bfloat16
U32 = jnp.uint32
HI = lax.Precision.HIGHEST


def _cp(*sem):
    return pltpu.CompilerParams(dimension_semantics=sem, vmem_limit_bytes=VMEM_LIMIT)


def _sigmoid(x):
    return 1.0 / (1.0 + jnp.exp(-x))


def _silu(x):
    return x * _sigmoid(x)


def _dot(a, b):
    return jnp.dot(a, b, preferred_element_type=F32)


def _dot_nt(a, b):
    return lax.dot_general(a, b, (((1,), (1,)), ((), ())), preferred_element_type=F32)


def _pack_rows(h):
    w = h.shape[1] // 2
    lo = lax.bitcast_convert_type(h[:, :w].astype(BF16).astype(F32), U32)
    hi = lax.bitcast_convert_type(h[:, w:].astype(BF16).astype(F32), U32)
    return (hi & jnp.uint32(0xFFFF0000)) | (lo >> 16)


def _unpack_rows(p):
    lo = lax.bitcast_convert_type(p << 16, F32)
    hi = lax.bitcast_convert_type(p & jnp.uint32(0xFFFF0000), F32)
    return lo, hi


def _ln_mod(v, lng, lnb, shift, scale):
    mu = jnp.mean(v, axis=-1, keepdims=True)
    d = v - mu
    var = jnp.mean(d * d, axis=-1, keepdims=True)
    xn = d * lax.rsqrt(var + LN_EPS) * lng + lnb
    return xn, xn * (1.0 + scale) + shift


def _adaln_kernel(c_ref, w1_ref, w2_ref, b_ref, o_ref):
    a = _silu(c_ref[...])
    t = jnp.dot(a, w1_ref[0], precision=HI, preferred_element_type=F32)
    o_ref[0] = jnp.dot(t, w2_ref[0], precision=HI, preferred_element_type=F32) + b_ref[0]


def _adaln(cond, w1, w2, b):
    depth, d, r = w1.shape
    n6 = w2.shape[2]
    tn = min(n6, 2048)
    rows = cond.shape[0]
    return pl.pallas_call(
        _adaln_kernel,
        grid=(depth, n6 // tn),
        in_specs=[
            pl.BlockSpec((rows, d), lambda l, j: (0, 0)),
            pl.BlockSpec((1, d, r), lambda l, j: (l, 0, 0)),
            pl.BlockSpec((1, r, tn), lambda l, j: (l, 0, j)),
            pl.BlockSpec((1, 1, tn), lambda l, j: (l, 0, j)),
        ],
        out_specs=pl.BlockSpec((1, rows, tn), lambda l, j: (l, 0, j)),
        out_shape=jax.ShapeDtypeStruct((depth, rows, n6), F32),
        compiler_params=_cp("arbitrary", "arbitrary"),
        name="adaln",
    )(cond, w1, w2, b.reshape(depth, 1, n6))


def _seg_fn(nt, nl):
    return lambda i: 2 * (i // nt) + (i % nt) // nl


def _modulate_kernel(x_ref, sh_ref, sc_ref, h_ref):
    h_ref[...] = (x_ref[...] * (1.0 + sc_ref[0]) + sh_ref[0]).astype(h_ref.dtype)


def _modulate(x, shift, scale, seg):
    n, d = x.shape
    row = pl.BlockSpec((TM, d), lambda i: (i, 0))
    vec = pl.BlockSpec((1, 1, d), lambda i: (seg(i), 0, 0))
    return pl.pallas_call(
        _modulate_kernel,
        grid=(n // TM,),
        in_specs=[row, vec, vec],
        out_specs=row,
        out_shape=jax.ShapeDtypeStruct((n, d), BF16),
        compiler_params=_cp("arbitrary"),
        name="modulate",
    )(x, shift, scale)


def _resid_ln_kernel(alpha, packed, x_ref, y_ref, g_ref, lng_ref, lnb_ref, sh_ref, sc_ref, xo_ref, h_ref):
    v = alpha * x_ref[...] + g_ref[0] * y_ref[...].astype(F32)
    xn, h = _ln_mod(v, lng_ref[0], lnb_ref[0], sh_ref[0], sc_ref[0])
    xo_ref[...] = xn
    h_ref[...] = _pack_rows(h) if packed else h.astype(h_ref.dtype)


def _resid_ln(x, y, gate, lng, lnb, shift, scale, seg, alpha, packed):
    n, d = x.shape
    row = pl.BlockSpec((TM, d), lambda i: (i, 0))
    vec = pl.BlockSpec((1, 1, d), lambda i: (seg(i), 0, 0))
    one = pl.BlockSpec((1, 1, d), lambda i: (0, 0, 0))
    if packed:
        h_spec, h_shape = pl.BlockSpec((TM, d // 2), lambda i: (i, 0)), jax.ShapeDtypeStruct((n, d // 2), U32)
    else:
        h_spec, h_shape = row, jax.ShapeDtypeStruct((n, d), BF16)
    return pl.pallas_call(
        functools.partial(_resid_ln_kernel, alpha, packed),
        grid=(n // TM,),
        in_specs=[row, row, vec, one, one, vec, vec],
        out_specs=[row, h_spec],
        out_shape=[jax.ShapeDtypeStruct((n, d), F32), h_shape],
        compiler_params=_cp("arbitrary"),
        name="resid_ln",
    )(x, y, gate, lng, lnb, shift, scale)


def _mm_kernel(x_ref, w_ref, o_ref, wbf_ref):
    @pl.when(pl.program_id(1) == 0)
    def _():
        wbf_ref[...] = w_ref[0].astype(BF16)

    o_ref[...] = _dot(x_ref[...], wbf_ref[...]).astype(o_ref.dtype)


def _row_tile(n):
    for tm in (768, 512, 256):
        if n % tm == 0:
            return tm
    raise ValueError(f"no row tile divides {n}")


def _mm(x, w, layer, out_dtype, tn):
    n, k = x.shape
    m = w.shape[2]
    tn = min(tn, m)
    tm = _row_tile(n)
    assert m % tn == 0
    return pl.pallas_call(
        _mm_kernel,
        grid=(m // tn, n // tm),
        in_specs=[
            pl.BlockSpec((tm, k), lambda j, i: (i, 0)),
            pl.BlockSpec((1, k, tn), lambda j, i: (layer, 0, j)),
        ],
        out_specs=pl.BlockSpec((tm, tn), lambda j, i: (i, j)),
        out_shape=jax.ShapeDtypeStruct((n, m), out_dtype),
        scratch_shapes=[pltpu.VMEM((k, tn), BF16)],
        compiler_params=_cp("arbitrary", "arbitrary"),
        name="mm",
    )(x, w)


NA_HG = 2
NA_GROUP = 4
NA_UNION = NA_KH + NA_GROUP - 1


def _na_group_geometry(rows):
    cases, ids = [], []
    for g in range(rows // NA_GROUP):
        r0 = g * NA_GROUP
        u_start = int(np.clip(r0 - NA_KH // 2, 0, rows - NA_UNION))
        geo = []
        for u in range(NA_GROUP):
            r = r0 + u
            r_start = int(np.clip(r - NA_KH // 2, 0, rows - NA_KH))
            assert u_start <= r_start and r_start + NA_KH <= u_start + NA_UNION
            geo.append((r - u_start, r - r_start))
        geo = tuple(geo)
        if geo not in cases:
            cases.append(geo)
        ids.append(cases.index(geo))
    return cases, np.asarray(ids, np.int32)


def _na_bias_tables(rpb, cases):
    col = np.arange(GRID_W)
    c_start = np.clip(col - NA_KW // 2, 0, GRID_W - NA_KW)
    col_ok = (col[None, :] >= c_start[:, None]) & (col[None, :] < c_start[:, None] + NA_KW)
    dc = np.clip(col[None, :] - col[:, None], 1 - NA_KW, NA_KW - 1) + NA_KW - 1
    onehot = (dc[:, None, :] == np.arange(2 * NA_KW - 1)[None, :, None]).astype(np.float32)
    t15 = jnp.einsum("hrd,qdk->hrqk", rpb, jnp.asarray(onehot), precision=HI)
    t15 = jnp.where(col_ok[None, None], t15, NEG_INF)
    neg = jnp.full((rpb.shape[0], GRID_W, GRID_W), NEG_INF, F32)
    tabs = []
    for geo in cases:
        blocks = []
        for e, dl in geo:
            tiles = [t15[:, a - e + NA_KH - 1] if 0 <= a - (e - dl) < NA_KH else neg for a in range(NA_UNION)]
            blocks.append(jnp.concatenate(tiles, axis=-1))
        tabs.append(jnp.concatenate(blocks, axis=1))
    return jnp.stack(tabs, axis=1)


def _na_kernel(s_lat, c_ctx, hd, case_ref, q_ref, k_ref, v_ref, bias_ref, o_ref):
    rows = s_lat // GRID_W
    gq = NA_GROUP * GRID_W
    gk = NA_UNION * GRID_W
    scale = hd ** -0.5
    ctx = slice(s_lat, s_lat + c_ctx)

    def group(g, hh):
        ln = slice(hh * hd, (hh + 1) * hd)
        u_start = jnp.clip(g * NA_GROUP - NA_KH // 2, 0, rows - NA_UNION)
        q0 = pl.multiple_of(g * gq, gq)
        k0 = pl.multiple_of(u_start * GRID_W, GRID_W)
        q_g = q_ref[pl.ds(q0, gq), ln]
        s_band = _dot_nt(q_g, k_ref[pl.ds(k0, gk), ln]) * scale + bias_ref[hh, case_ref[g]]
        s_ctx = _dot_nt(q_g, k_ref[ctx, ln]) * scale
        m = jnp.maximum(jnp.max(s_band, axis=-1, keepdims=True), jnp.max(s_ctx, axis=-1, keepdims=True))
        p_band = jnp.exp(s_band - m)
        p_ctx = jnp.exp(s_ctx - m)
        den = jnp.sum(p_band, axis=-1, keepdims=True) + jnp.sum(p_ctx, axis=-1, keepdims=True)
        o = _dot(p_band.astype(BF16), v_ref[pl.ds(k0, gk), ln]) + _dot(p_ctx.astype(BF16), v_ref[ctx, ln])
        o_ref[pl.ds(q0, gq), ln] = (o / den).astype(o_ref.dtype)

    def body(g, carry):
        for hh in range(NA_HG):
            group(g, hh)
        return carry

    lax.fori_loop(0, rows // NA_GROUP, body, 0)
    for hh in range(NA_HG):
        ln = slice(hh * hd, (hh + 1) * hd)
        s = _dot_nt(q_ref[ctx, ln], k_ref[ctx, ln]) * scale
        p = jnp.exp(s - jnp.max(s, axis=-1, keepdims=True))
        o = _dot(p.astype(BF16), v_ref[ctx, ln]) / jnp.sum(p, axis=-1, keepdims=True)
        o_ref[ctx, ln] = o.astype(o_ref.dtype)


def _na_attention(qkv, rpb, batch, s_lat, c_ctx, d):
    r = s_lat + c_ctx
    hd = d // NA_HEADS
    wb = NA_HG * hd
    nhg = NA_HEADS // NA_HG
    rows = s_lat // GRID_W
    assert rows % NA_GROUP == 0 and rows >= NA_UNION
    cases, case_ids = _na_group_geometry(rows)
    bias = _na_bias_tables(rpb, cases)
    grid_spec = pltpu.PrefetchScalarGridSpec(
        num_scalar_prefetch=1,
        grid=(batch, nhg),
        in_specs=[
            pl.BlockSpec((r, wb), lambda b, g, c: (b, g)),
            pl.BlockSpec((r, wb), lambda b, g, c: (b, nhg + g)),
            pl.BlockSpec((r, wb), lambda b, g, c: (b, 2 * nhg + g)),
            pl.BlockSpec((NA_HG,) + bias.shape[1:], lambda b, g, c: (g, 0, 0, 0)),
        ],
        out_specs=pl.BlockSpec((r, wb), lambda b, g, c: (b, g)),
    )
    return pl.pallas_call(
        functools.partial(_na_kernel, s_lat, c_ctx, hd),
        grid_spec=grid_spec,
        out_shape=jax.ShapeDtypeStruct((batch * r, d), BF16),
        compiler_params=_cp("arbitrary", "arbitrary"),
        name="na_attention",
    )(jnp.asarray(case_ids), qkv, qkv, qkv, bias)


def _rope_tables(s_lat, c_ctx, dk):
    half = dk // 2
    nf = half // 2
    pos = np.arange(s_lat)
    inv = ROPE_BASE ** (-np.arange(nf, dtype=np.float32) / nf)
    ang_r = (pos // GRID_W).astype(np.float32)[:, None] * inv
    ang_c = (pos % GRID_W).astype(np.float32)[:, None] * inv
    cos = np.concatenate([np.cos(ang_r), np.cos(ang_r), np.cos(ang_c), np.cos(ang_c)], axis=1)
    sin = np.concatenate([-np.sin(ang_r), np.sin(ang_r), -np.sin(ang_c), np.sin(ang_c)], axis=1)
    cos = np.concatenate([cos, np.ones((c_ctx, dk), np.float32)], axis=0)
    sin = np.concatenate([sin, np.zeros((c_ctx, dk), np.float32)], axis=0)
    return jnp.asarray(cos, F32), jnp.asarray(sin, F32)


def _swap_quarters(u):
    parts = [pltpu.roll(u[:, i:i + LANES], LANES // 2, 1) for i in range(0, u.shape[1], LANES)]
    return jnp.concatenate(parts, axis=1)


def _split2(a):
    hi = a.astype(BF16)
    return hi, (a - hi.astype(F32)).astype(BF16)


def _gla_block(dk, dirn, q_ref, k_ref, v_ref, r_ref, cos_ref, sin_ref, w2_ref, ba_ref, o_ref, st_ref):
    L = GLA_CHUNK
    tb = q_ref.shape[0]
    nch = tb // L
    row = lax.broadcasted_iota(jnp.int32, (tb, tb), 0)
    col = lax.broadcasted_iota(jnp.int32, (tb, tb), 1)
    tri = ((row // L) == (col // L)) & ((row >= col) if dirn == 0 else (row <= col))
    cos = cos_ref[...]
    sin = sin_ref[...]
    q = q_ref[...].astype(F32) * (dk ** -0.5)
    k = k_ref[...].astype(F32)
    q = q * cos + _swap_quarters(q) * sin
    k = k * cos + _swap_quarters(k) * sin
    v = v_ref[...]
    r_hi, r_lo = _split2(r_ref[...])
    pre = _dot(jnp.concatenate([r_hi, r_lo, r_hi], axis=1), w2_ref[dirn, 0]) + ba_ref[dirn, 0]
    la = (jnp.minimum(pre, 0.0) - jnp.log(1.0 + jnp.exp(-jnp.abs(pre)))) / GLA_GATE_NORMALIZER
    la_hi = la.astype(BF16)
    la_mid, la_lo = _split2(la - la_hi.astype(F32))
    b3 = _dot(jnp.where(tri, 1.0, 0.0).astype(BF16), jnp.concatenate([la_hi, la_mid, la_lo], axis=1))
    b = b3[:, :dk] + b3[:, dk:2 * dk] + b3[:, 2 * dk:]
    qe, ke, qd, kd, decay = [], [], [], [], []
    for c in range(nch):
        sl = slice(c * L, (c + 1) * L)
        bc = b[sl]
        mc = bc[L // 2:L // 2 + 1]
        blc = bc[L - 1:L] if dirn == 0 else bc[0:1]
        qe.append((q[sl] * jnp.exp(bc - mc)).astype(BF16))
        ke.append((k[sl] * jnp.exp(mc - bc)).astype(BF16))
        qd.append((q[sl] * jnp.exp(bc)).astype(BF16))
        kd.append((k[sl] * jnp.exp(blc - bc)).astype(BF16))
        decay.append(jnp.exp(blc))
    attn = _dot_nt(jnp.concatenate(qe, axis=0), jnp.concatenate(ke, axis=0))
    intra = _dot(jnp.where(tri, attn, 0.0).astype(BF16), v)
    for c in (range(nch) if dirn == 0 else range(nch - 1, -1, -1)):
        sl = slice(c * L, (c + 1) * L)
        st = st_ref[dirn]
        o_ref[sl, :] = (intra[sl] + _dot_nt(qd[c], st.astype(BF16))).astype(o_ref.dtype)
        upd = lax.dot_general(v[sl], kd[c], (((0,), (0,)), ((), ())), preferred_element_type=F32)
        st_ref[dirn] = decay[c] * st + upd


def _gla_scan_kernel(dk, qf, kf, vf, rf, cosf, sinf, qb, kb, vb, rb, cosb, sinb, w2_ref, ba_ref, of_ref, ob_ref,
                     st_ref):
    @pl.when(pl.program_id(2) == 0)
    def _():
        st_ref[...] = jnp.zeros_like(st_ref)

    _gla_block(dk, 0, qf, kf, vf, rf, cosf, sinf, w2_ref, ba_ref, of_ref, st_ref)
    _gla_block(dk, 1, qb, kb, vb, rb, cosb, sinb, w2_ref, ba_ref, ob_ref, st_ref)


def _gla_scan(z, r, w2pad, ba, cos, sin, batch, s_lat, c_ctx, d):
    n = z.shape[0]
    hg = GLA_HEADS
    dkt = d // 2
    dk, dv = dkt // hg, d // hg
    nl = s_lat // TM
    nt = nl + 1
    assert c_ctx == TM

    blks = (lambda s: jnp.where(s == 0, nl, s - 1), lambda s: jnp.where(s == 0, nl, nl - s))
    in_specs = []
    for blk in blks:
        in_specs += [
            pl.BlockSpec((TM, dk), lambda b, h, s, blk=blk: (b * nt + blk(s), h)),
            pl.BlockSpec((TM, dk), lambda b, h, s, blk=blk: (b * nt + blk(s), hg + h)),
            pl.BlockSpec((TM, dv), lambda b, h, s, blk=blk: (b * nt + blk(s), 2 * dkt // dv + h)),
            pl.BlockSpec((TM, LANES), lambda b, h, s, blk=blk: (b * nt + blk(s), 0)),
            pl.BlockSpec((TM, dk), lambda b, h, s, blk=blk: (blk(s), 0)),
            pl.BlockSpec((TM, dk), lambda b, h, s, blk=blk: (blk(s), 0)),
        ]
    in_specs += [
        pl.BlockSpec((2, 1, 3 * LANES, dk), lambda b, h, s: (0, h, 0, 0)),
        pl.BlockSpec((2, 1, 1, dk), lambda b, h, s: (0, h, 0, 0)),
    ]
    out_specs = [pl.BlockSpec((TM, dv), lambda b, h, s, blk=blk: (b * nt + blk(s), h)) for blk in blks]
    return pl.pallas_call(
        functools.partial(_gla_scan_kernel, dk),
        grid=(batch, hg, nt),
        in_specs=in_specs,
        out_specs=out_specs,
        out_shape=[jax.ShapeDtypeStruct((n, d), F32), jax.ShapeDtypeStruct((n, d), F32)],
        scratch_shapes=[pltpu.VMEM((2, dv, dk), F32)],
        compiler_params=_cp("arbitrary", "arbitrary", "arbitrary"),
        name="gla_scan",
    )(z, z, z, r, cos, sin, z, z, z, r, cos, sin, w2pad, ba)


def _gla_out_kernel(dv, of_ref, ob_ref, g_ref, nw_ref, h_ref):
    nw = nw_ref[0]
    for h in range(of_ref.shape[1] // dv):
        sl = slice(h * dv, (h + 1) * dv)
        o = of_ref[:, sl] + ob_ref[:, sl]
        o = o * lax.rsqrt(jnp.mean(o * o, axis=-1, keepdims=True) + LN_EPS) * nw
        h_ref[:, sl] = (o * _silu(g_ref[:, sl].astype(F32))).astype(h_ref.dtype)


def _gla_out(o_f, o_b, z, norm_w, layer, d):
    n = z.shape[0]
    dv = d // GLA_HEADS
    gcol = (z.shape[1] - d) // d
    row = pl.BlockSpec((TM, d), lambda i: (i, 0))
    return pl.pallas_call(
        functools.partial(_gla_out_kernel, dv),
        grid=(n // TM,),
        in_specs=[
            row, row,
            pl.BlockSpec((TM, d), lambda i: (i, gcol)),
            pl.BlockSpec((1, 1, dv), lambda i: (layer, 0, 0)),
        ],
        out_specs=row,
        out_shape=jax.ShapeDtypeStruct((n, d), BF16),
        compiler_params=_cp("arbitrary"),
        name="gla_out",
    )(o_f, o_b, z, norm_w.reshape(norm_w.shape[0], 1, dv))


def _rank_desc(v, axis_len):
    idx = lax.broadcasted_iota(jnp.int32, v.shape, 0)
    rank = jnp.zeros(v.shape, jnp.int32)
    for j in range(axis_len):
        vj = v[j:j + 1]
        rank = rank + jnp.where((vj > v) | ((vj == v) & (j < idx)), 1, 0)
    return rank


def _router_kernel(n_e, x_ref, sh_ref, sc_ref, wr_ref, eb_ref, eid_ref, rnk_ref, gw_ref, cnt_ref, run_ref):
    @pl.when(pl.program_id(0) == 0)
    def _():
        run_ref[...] = jnp.zeros_like(run_ref)

    tm = x_ref.shape[0]
    gsz = n_e // N_GROUPS
    h = x_ref[...] * (1.0 + sc_ref[0]) + sh_ref[0]
    logits = lax.dot_general(wr_ref[0], h, (((1,), (1,)), ((), ())), precision=HI, preferred_element_type=F32)
    scores = _sigmoid(logits)
    biased = scores + eb_ref[0]
    grp = biased.reshape(N_GROUPS, gsz, tm)
    m1 = jnp.max(grp, axis=1, keepdims=True)
    gi = lax.broadcasted_iota(jnp.int32, grp.shape, 1)
    first = jnp.min(jnp.where(grp == m1, gi, gsz), axis=1, keepdims=True)
    m2 = jnp.max(jnp.where(gi == first, -jnp.inf, grp), axis=1, keepdims=True)
    g_sel = _rank_desc(m1 + m2, N_GROUPS) < TOPK_GROUPS
    e_sel = jnp.broadcast_to(g_sel, (N_GROUPS, gsz, tm)).reshape(n_e, tm)
    masked = jnp.where(e_sel, biased, NEG_INF)
    eidx_i = lax.broadcasted_iota(jnp.int32, (n_e, tm), 0)
    firsts, hits = [], []
    sel_f = jnp.zeros((n_e, tm), F32)
    for _ in range(TOP_K):
        mx = jnp.max(masked, axis=0, keepdims=True)
        first = jnp.min(jnp.where(masked == mx, eidx_i, n_e), axis=0, keepdims=True)
        hit = eidx_i == first
        firsts.append(first)
        hits.append(hit)
        sel_f = sel_f + jnp.where(hit, 1.0, 0.0)
        masked = jnp.where(hit, -jnp.inf, masked)
    w = scores * sel_f
    gates = w / jnp.sum(w, axis=0, keepdims=True) * ROUTED_SCALE
    t0 = lax.broadcasted_iota(jnp.int32, (tm, tm), 0)
    t1 = lax.broadcasted_iota(jnp.int32, (tm, tm), 1)
    before = _dot(sel_f.astype(BF16), jnp.where(t0 < t1, 1.0, 0.0).astype(BF16))
    pos = run_ref[:, 0:1] + before
    run_ref[...] = run_ref[...] + jnp.sum(sel_f, axis=1, keepdims=True)
    cnt_ref[...] = run_ref[...].astype(jnp.int32)
    for j in range(TOP_K):
        eid_ref[j:j + 1, :] = firsts[j]
        rnk_ref[j:j + 1, :] = jnp.sum(jnp.where(hits[j], pos, 0.0), axis=0, keepdims=True).astype(jnp.int32)
        gw_ref[j:j + 1, :] = jnp.sum(jnp.where(hits[j], gates, 0.0), axis=0, keepdims=True)


def _router(x, shift, scale, w_router_t, e_bias, layer, seg):
    n, d = x.shape
    n_e = w_router_t.shape[1]
    vec = pl.BlockSpec((1, 1, d), lambda i: (seg(i), 0, 0))
    sel = pl.BlockSpec((TOP_K, TM), lambda i: (0, i))
    return pl.pallas_call(
        functools.partial(_router_kernel, n_e),
        grid=(n // TM,),
        in_specs=[
            pl.BlockSpec((TM, d), lambda i: (i, 0)),
            vec, vec,
            pl.BlockSpec((1, n_e, d), lambda i: (layer, 0, 0)),
            pl.BlockSpec((1, n_e, 1), lambda i: (layer, 0, 0)),
        ],
        out_specs=[sel, sel, sel, pl.BlockSpec((n_e, LANES), lambda i: (0, 0))],
        out_shape=[
            jax.ShapeDtypeStruct((TOP_K, n), jnp.int32),
            jax.ShapeDtypeStruct((TOP_K, n), jnp.int32),
            jax.ShapeDtypeStruct((TOP_K, n), F32),
            jax.ShapeDtypeStruct((n_e, LANES), jnp.int32),
        ],
        scratch_shapes=[pltpu.VMEM((n_e, LANES), F32)],
        compiler_params=_cp("arbitrary"),
        name="router",
    )(x, shift, scale, w_router_t, e_bias)


ROW_UNROLL = 8
GROUP_UNROLL = 8


def _gather_rows(idx_ref, src_ref, dst_ref, sem, n_rows):
    def group(g, c):
        for v in range(GROUP_UNROLL):
            for u in range(ROW_UNROLL):
                t = idx_ref[0, 0, (g * GROUP_UNROLL + v) * ROW_UNROLL + u]
                src = src_ref.at[lax.shift_right_logical(t, 3), pl.ds(t & 7, 1)]
                pltpu.make_async_copy(src, dst_ref.at[g * GROUP_UNROLL + v, pl.ds(u, 1)], sem).start(priority=u % 2)
        return c

    lax.fori_loop(0, n_rows // (ROW_UNROLL * GROUP_UNROLL), group, 0)


def _wait_rows(src_ref, dst_ref, sem, n_rows):
    pltpu.make_async_copy(src_ref.at[pl.ds(0, n_rows // ROW_UNROLL)], dst_ref, sem).wait()


GATHER_AHEAD = 2


def _request_rows(step, n_valid, first_refs, ahead_ref, src_ref, buf, sem, n_rows):
    for a, ref in enumerate(first_refs):
        @pl.when((step == 0) & (n_valid > a))
        def _(a=a, ref=ref):
            _gather_rows(ref, src_ref, buf.at[a], sem.at[a], n_rows)

    @pl.when(step + GATHER_AHEAD < n_valid)
    def _():
        s = lax.rem(step + GATHER_AHEAD, GATHER_AHEAD + 1)
        _gather_rows(ahead_ref, src_ref, buf.at[s], sem.at[s], n_rows)


def _expert_kernel(te_ref, nt_ref, tok0_ref, tok1_ref, tokn_ref, hp_ref, wg_ref, wu_ref, wd_ref, y_ref, xbuf, wgb,
                   wub, wdb, sem):
    j = pl.program_id(0)
    nt = nt_ref[0]
    tg = xbuf.shape[1] * xbuf.shape[2]
    slot = lax.rem(j, GATHER_AHEAD + 1)
    _request_rows(j, nt, (tok0_ref, tok1_ref), tokn_ref, hp_ref, xbuf, sem, tg)

    @pl.when(j < nt)
    def _():
        @pl.when((j == 0) | (te_ref[j] != te_ref[jnp.maximum(j - 1, 0)]))
        def _():
            wgb[...] = wg_ref[0, 0].astype(BF16)
            wub[...] = wu_ref[0, 0].astype(BF16)
            wdb[...] = wd_ref[0, 0].astype(BF16)

        _wait_rows(hp_ref, xbuf.at[slot], sem.at[slot], tg)
        lo, hi = _unpack_rows(xbuf[slot].reshape(tg, xbuf.shape[3]))
        x = jnp.concatenate([lo.astype(BF16), hi.astype(BF16)], axis=1)
        hid = _silu(_dot_nt(x, wgb[...])) * _dot_nt(x, wub[...])
        y_ref[...] = _pack_rows(_dot(hid.astype(BF16), wdb[...]))

    @pl.when(j >= nt)
    def _():
        y_ref[...] = jnp.zeros_like(y_ref)


def _experts(hp, tok, tile_eid, n_tiles_used, w_gate_t, w_up_t, w_down, layer):
    n, dh = hp.shape
    d = 2 * dh
    f = w_gate_t.shape[2]
    n_tiles = tok.shape[0]
    assert n % ROW_UNROLL == 0 and TG % ROW_UNROLL == 0
    grid_spec = pltpu.PrefetchScalarGridSpec(
        num_scalar_prefetch=2,
        grid=(n_tiles,),
        in_specs=[
            pl.BlockSpec((1, 1, TG), lambda j, te, nt: (0, 0, 0), memory_space=pltpu.SMEM),
            pl.BlockSpec((1, 1, TG), lambda j, te, nt: (1, 0, 0), memory_space=pltpu.SMEM),
            pl.BlockSpec((1, 1, TG), lambda j, te, nt: (jnp.minimum(j + GATHER_AHEAD, n_tiles - 1), 0, 0),
                         memory_space=pltpu.SMEM),
            pl.BlockSpec(memory_space=pl.ANY),
            pl.BlockSpec((1, 1, f, d), lambda j, te, nt: (layer, te[j], 0, 0)),
            pl.BlockSpec((1, 1, f, d), lambda j, te, nt: (layer, te[j], 0, 0)),
            pl.BlockSpec((1, 1, f, d), lambda j, te, nt: (layer, te[j], 0, 0)),
        ],
        out_specs=pl.BlockSpec((TG, dh), lambda j, te, nt: (j, 0)),
        scratch_shapes=[
            pltpu.VMEM((GATHER_AHEAD + 1, TG // ROW_UNROLL, ROW_UNROLL, dh), U32),
            pltpu.VMEM((f, d), BF16),
            pltpu.VMEM((f, d), BF16),
            pltpu.VMEM((f, d), BF16),
            pltpu.SemaphoreType.DMA((GATHER_AHEAD + 1,)),
        ],
    )
    return pl.pallas_call(
        _expert_kernel,
        grid_spec=grid_spec,
        out_shape=jax.ShapeDtypeStruct((n_tiles * TG, dh), U32),
        compiler_params=_cp("arbitrary"),
        name="experts",
    )(tile_eid, n_tiles_used, tok, tok, tok, hp.reshape(n // ROW_UNROLL, ROW_UNROLL, dh), w_gate_t, w_up_t, w_down)


def _shared_kernel(hp_ref, sg_ref, su_ref, sd_ref, y_ref, sgb, sub, sdb):
    @pl.when(pl.program_id(0) == 0)
    def _():
        sgb[...] = sg_ref[0].astype(BF16)
        sub[...] = su_ref[0].astype(BF16)
        sdb[...] = sd_ref[0].astype(BF16)

    lo, hi = _unpack_rows(hp_ref[...])
    x = jnp.concatenate([lo.astype(BF16), hi.astype(BF16)], axis=1)
    hid = _silu(_dot_nt(x, sgb[...])) * _dot_nt(x, sub[...])
    y_ref[...] = _dot(hid.astype(BF16), sdb[...]).astype(y_ref.dtype)


def _shared_expert(hp, sg_t, su_t, sd, layer):
    n, dh = hp.shape
    d = 2 * dh
    f = sg_t.shape[1]
    tm = _row_tile(n)
    wspec = pl.BlockSpec((1, f, d), lambda i: (layer, 0, 0))
    return pl.pallas_call(
        _shared_kernel,
        grid=(n // tm,),
        in_specs=[pl.BlockSpec((tm, dh), lambda i: (i, 0)), wspec, wspec, wspec],
        out_specs=pl.BlockSpec((tm, d), lambda i: (i, 0)),
        out_shape=jax.ShapeDtypeStruct((n, d), BF16),
        scratch_shapes=[pltpu.VMEM((f, d), BF16), pltpu.VMEM((f, d), BF16), pltpu.VMEM((f, d), BF16)],
        compiler_params=_cp("arbitrary"),
        name="shared_expert",
    )(hp, sg_t, su_t, sd)


TMC = 128


def _combine_kernel(alpha, dest0_ref, dest1_ref, destn_ref, ys_ref, ysh_ref, gw_ref, x_ref, g_ref, lng_ref, lnb_ref,
                    sh_ref, sc_ref, xo_ref, h_ref, buf, sem):
    tm, d = x_ref.shape
    dh = d // 2
    i = pl.program_id(0)
    slot = lax.rem(i, GATHER_AHEAD + 1)
    n_rows = TOP_K * tm
    _request_rows(i, pl.num_programs(0), (dest0_ref, dest1_ref), destn_ref, ys_ref, buf, sem, n_rows)
    _wait_rows(ys_ref, buf.at[slot], sem.at[slot], n_rows)
    ysh = ysh_ref[...].astype(F32)
    y_lo, y_hi = ysh[:, :dh], ysh[:, dh:]
    tg = tm // ROW_UNROLL
    for kk in range(TOP_K):
        lo, hi = _unpack_rows(buf[slot, kk * tg:(kk + 1) * tg].reshape(tm, dh))
        w = gw_ref[:, kk:kk + 1]
        y_lo = y_lo + w * lo
        y_hi = y_hi + w * hi
    y = jnp.concatenate([y_lo, y_hi], axis=1)
    v = alpha * x_ref[...] + g_ref[0] * y
    xn, h = _ln_mod(v, lng_ref[0], lnb_ref[0], sh_ref[0], sc_ref[0])
    xo_ref[...] = xn
    h_ref[...] = h.astype(h_ref.dtype)


def _combine(dest, ys, ysh, gw, x, gate, lng, lnb, shift, scale, seg, alpha):
    n, d = x.shape
    row = pl.BlockSpec((TMC, d), lambda i: (i, 0))
    vec = pl.BlockSpec((1, 1, d), lambda i: (seg(i), 0, 0))
    one = pl.BlockSpec((1, 1, d), lambda i: (0, 0, 0))
    n_tiles = n // TMC
    return pl.pallas_call(
        functools.partial(_combine_kernel, alpha),
        grid=(n_tiles,),
        in_specs=[
            pl.BlockSpec((1, 1, TOP_K * TMC), lambda i: (0, 0, 0), memory_space=pltpu.SMEM),
            pl.BlockSpec((1, 1, TOP_K * TMC), lambda i: (1, 0, 0), memory_space=pltpu.SMEM),
            pl.BlockSpec((1, 1, TOP_K * TMC), lambda i: (jnp.minimum(i + GATHER_AHEAD, n_tiles - 1), 0, 0),
                         memory_space=pltpu.SMEM),
            pl.BlockSpec(memory_space=pl.ANY),
            row,
            pl.BlockSpec((TMC, TOP_K), lambda i: (i, 0)),
            row, vec, one, one, vec, vec,
        ],
        out_specs=[row, row],
        out_shape=[jax.ShapeDtypeStruct((n, d), F32), jax.ShapeDtypeStruct((n, d), BF16)],
        scratch_shapes=[pltpu.VMEM((GATHER_AHEAD + 1, TOP_K * TMC // ROW_UNROLL, ROW_UNROLL, d // 2), U32),
                        pltpu.SemaphoreType.DMA((GATHER_AHEAD + 1,))],
        compiler_params=_cp("arbitrary"),
        name="moe_combine",
    )(dest, dest, dest, ys.reshape(ys.shape[0] // ROW_UNROLL, ROW_UNROLL, d // 2), ysh, gw, x, gate, lng, lnb, shift, scale)


def _moe(x, hp, mods_f, w_router_t, e_bias, w_gate, w_up, w_down, sg, su, sd, gate, lng, lnb, nshift, nscale,
         layer, seg, seg_c, alpha):
    n, d = x.shape
    n_e = w_router_t.shape[1]
    shift_f, scale_f = mods_f
    eid, rnk, gw, cnt = _router(x, shift_f, scale_f, w_router_t, e_bias, layer, seg)
    counts = cnt[:, 0]
    tiles_e = (counts + TG - 1) // TG
    tile_end = jnp.cumsum(tiles_e)
    n_tiles = (n * TOP_K) // TG + n_e
    tile_ids = jnp.arange(n_tiles, dtype=jnp.int32)
    tile_eid = jnp.minimum(jnp.sum((tile_end[None, :] <= tile_ids[:, None]).astype(jnp.int32), axis=1), n_e - 1)
    row_off = (tile_end - tiles_e) * TG
    e_ids = jnp.arange(n_e, dtype=jnp.int32)
    dest = jnp.sum(jnp.where(eid[..., None] == e_ids, row_off, 0), axis=-1) + rnk
    tok = jnp.zeros((n_tiles * TG,), jnp.int32).at[dest.reshape(-1)].set(
        jnp.broadcast_to(jnp.arange(n, dtype=jnp.int32), (TOP_K, n)).reshape(-1),
        unique_indices=True, mode="promise_in_bounds")
    ys = _experts(hp, tok.reshape(n_tiles, 1, TG), tile_eid, tile_end[-1:].astype(jnp.int32), w_gate, w_up, w_down,
                  layer)
    ysh = _shared_expert(hp, sg, su, sd, layer)
    dest_t = dest.reshape(TOP_K, n // TMC, TMC).transpose(1, 0, 2).reshape(n // TMC, 1, TOP_K * TMC)
    return _combine(dest_t, ys, ysh, gw.T, x, gate, lng, lnb, nshift, nscale, seg_c, alpha)


def kernel(x, c, ctx, c_ctx, ada_w1, ada_w2, ada_b, ln_g, ln_b, na_w_qkv, na_w_o, na_rpb, gla_w_in, gla_w_a1,
           gla_w_a2, gla_b_a, gla_norm, gla_w_o, moe_router, moe_bias, moe_w_gate, moe_w_up, moe_w_down,
           sh_w_gate, sh_w_up, sh_w_down):
    batch, s_lat, d = x.shape
    c_ctx_len = ctx.shape[1]
    depth = ada_w1.shape[0]
    alpha = (2 * depth) ** 0.25
    assert s_lat % TM == 0 and c_ctx_len == TM and s_lat % GRID_W == 0
    r = s_lat + c_ctx_len
    n = batch * r
    nl, nt = s_lat // TM, r // TM
    seg = _seg_fn(nt, nl)
    seg_c = _seg_fn(nt * (TM // TMC), nl * (TM // TMC))

    xa = jnp.concatenate([x, ctx], axis=1).reshape(n, d)
    cond = jnp.stack([c, jnp.broadcast_to(c_ctx, c.shape)], axis=1).reshape(2 * batch, d)
    rows = -(-2 * batch // 8) * 8
    cond = jnp.pad(cond, ((0, rows - 2 * batch), (0, 0)))
    mods = _adaln(cond, ada_w1, ada_w2, ada_b).reshape(depth, rows, 6, d).transpose(0, 2, 1, 3)
    mods = mods.reshape(depth, 6, rows, 1, d)
    lng = ln_g.reshape(depth, 2, 1, 1, d)
    lnb = ln_b.reshape(depth, 2, 1, 1, d)

    hd_gla = d // 2 // GLA_HEADS
    cos, sin = _rope_tables(s_lat, c_ctx_len, hd_gla)
    n_e = moe_router.shape[2]
    w_router_t = jnp.swapaxes(moe_router, 1, 2)
    e_bias = moe_bias.reshape(depth, n_e, 1)
    w_gate_t, w_up_t = jnp.swapaxes(moe_w_gate, 2, 3), jnp.swapaxes(moe_w_up, 2, 3)
    sg_t, su_t = jnp.swapaxes(sh_w_gate, 1, 2), jnp.swapaxes(sh_w_up, 1, 2)
    rank_g = gla_w_a1.shape[3]
    assert 2 * rank_g <= LANES

    h = _modulate(xa, mods[0, 0], mods[0, 1], seg)
    for i in range(depth):
        last = i == depth - 1
        m = mods[i]
        j = i // 2
        if i % 2 == 0:
            qkv = _mm(h, na_w_qkv, j, BF16, 512)
            o = _na_attention(qkv, na_rpb[j], batch, s_lat, c_ctx_len, d)
            y = _mm(o, na_w_o, j, F32, 512)
        else:
            z = _mm(h, gla_w_in, j, BF16, 512)
            w_a1 = jnp.concatenate([gla_w_a1[j, 0], gla_w_a1[j, 1]], axis=1)
            w_a1 = jnp.pad(w_a1, ((0, 0), (0, LANES - 2 * rank_g)))[None]
            rr = _mm(h, w_a1, 0, F32, LANES)
            w2 = gla_w_a2[j].reshape(2, rank_g, GLA_HEADS, hd_gla).transpose(0, 2, 1, 3)
            w2pad = jnp.zeros((2, GLA_HEADS, LANES, hd_gla), F32)
            w2pad = w2pad.at[0, :, :rank_g].set(w2[0]).at[1, :, rank_g:2 * rank_g].set(w2[1])
            w2_hi = w2pad.astype(BF16)
            w2_lo = (w2pad - w2_hi.astype(F32)).astype(BF16)
            w2cat = jnp.concatenate([w2_hi, w2_hi, w2_lo], axis=2)
            ba = gla_b_a[j].reshape(2, GLA_HEADS, 1, hd_gla)
            o_f, o_b = _gla_scan(z, rr, w2cat, ba, cos, sin, batch, s_lat, c_ctx_len, d)
            og = _gla_out(o_f, o_b, z, gla_norm, j, d)
            y = _mm(og, gla_w_o, j, F32, 512)
        x1, hp = _resid_ln(xa, y, m[2], lng[i, 0], lnb[i, 0], m[3], m[4], seg, alpha, packed=True)
        nxt = mods[i + 1] if not last else m
        xa, h = _moe(x1, hp, (m[3], m[4]), w_router_t, e_bias, w_gate_t, w_up_t, moe_w_down,
                     sg_t, su_t, sh_w_down, m[5], lng[i, 1], lnb[i, 1], nxt[0], nxt[1],
                     i, seg, seg_c, alpha)
    return xa.reshape(batch, r, d)[:, :s_lat]
```

```python
import functools

import jax
import jax.numpy as jnp
import numpy as np
from jax import lax
from jax.experimental import pallas as pl
from jax.experimental.pallas import tpu as pltpu

GRID_W = 64
NA_HEADS = 32
NA_KH = 8
NA_KW = 16
GLA_HEADS = 8
GLA_GATE_NORMALIZER = 16.0
GLA_CHUNK = 64
ROPE_BASE = 10000.0
N_GROUPS = 8
TOPK_GROUPS = 4
TOP_K = 8
ROUTED_SCALE = 2.5
LN_EPS = 1e-6
NEG_INF = -1e30

LANES = 128
TM = 256
TG = 512
VMEM_LIMIT = 56 * 1024 * 1024

F32 = jnp.float32
BF16 = jnp.bfloat16
U32 = jnp.uint32
HI = lax.Precision.HIGHEST


def _cp(*sem):
    return pltpu.CompilerParams(dimension_semantics=sem, vmem_limit_bytes=VMEM_LIMIT)


def _sigmoid(x):
    return 1.0 / (1.0 + jnp.exp(-x))


def _silu(x):
    return x * _sigmoid(x)


def _dot(a, b):
    return jnp.dot(a, b, preferred_element_type=F32)


def _dot_nt(a, b):
    return lax.dot_general(a, b, (((1,), (1,)), ((), ())), preferred_element_type=F32)


def _pack_rows(h):
    w = h.shape[1] // 2
    lo = lax.bitcast_convert_type(h[:, :w].astype(BF16).astype(F32), U32)
    hi = lax.bitcast_convert_type(h[:, w:].astype(BF16).astype(F32), U32)
    return (hi & jnp.uint32(0xFFFF0000)) | (lo >> 16)


def _unpack_rows(p):
    lo = lax.bitcast_convert_type(p << 16, F32)
    hi = lax.bitcast_convert_type(p & jnp.uint32(0xFFFF0000), F32)
    return lo, hi


def _ln_mod(v, lng, lnb, shift, scale):
    mu = jnp.mean(v, axis=-1, keepdims=True)
    d = v - mu
    var = jnp.mean(d * d, axis=-1, keepdims=True)
    xn = d * lax.rsqrt(var + LN_EPS) * lng + lnb
    return xn, xn * (1.0 + scale) + shift


def _adaln_kernel(c_ref, w1_ref, w2_ref, b_ref, o_ref):
    a = _silu(c_ref[...])
    t = jnp.dot(a, w1_ref[0], precision=HI, preferred_element_type=F32)
    o_ref[0] = jnp.dot(t, w2_ref[0], precision=HI, preferred_element_type=F32) + b_ref[0]


def _adaln(cond, w1, w2, b):
    depth, d, r = w1.shape
    n6 = w2.shape[2]
    tn = min(n6, 2048)
    rows = cond.shape[0]
    return pl.pallas_call(
        _adaln_kernel,
        grid=(depth, n6 // tn),
        in_specs=[
            pl.BlockSpec((rows, d), lambda l, j: (0, 0)),
            pl.BlockSpec((1, d, r), lambda l, j: (l, 0, 0)),
            pl.BlockSpec((1, r, tn), lambda l, j: (l, 0, j)),
            pl.BlockSpec((1, 1, tn), lambda l, j: (l, 0, j)),
        ],
        out_specs=pl.BlockSpec((1, rows, tn), lambda l, j: (l, 0, j)),
        out_shape=jax.ShapeDtypeStruct((depth, rows, n6), F32),
        compiler_params=_cp("arbitrary", "arbitrary"),
        name="adaln",
    )(cond, w1, w2, b.reshape(depth, 1, n6))


def _seg_fn(nt, nl):
    return lambda i: 2 * (i // nt) + (i % nt) // nl


def _modulate_kernel(x_ref, sh_ref, sc_ref, h_ref):
    h_ref[...] = (x_ref[...] * (1.0 + sc_ref[0]) + sh_ref[0]).astype(h_ref.dtype)


def _modulate(x, shift, scale, seg):
    n, d = x.shape
    row = pl.BlockSpec((TM, d), lambda i: (i, 0))
    vec = pl.BlockSpec((1, 1, d), lambda i: (seg(i), 0, 0))
    return pl.pallas_call(
        _modulate_kernel,
        grid=(n // TM,),
        in_specs=[row, vec, vec],
        out_specs=row,
        out_shape=jax.ShapeDtypeStruct((n, d), BF16),
        compiler_params=_cp("arbitrary"),
        name="modulate",
    )(x, shift, scale)


def _resid_ln_kernel(alpha, packed, x_ref, y_ref, g_ref, lng_ref, lnb_ref, sh_ref, sc_ref, xo_ref, h_ref):
    v = alpha * x_ref[...] + g_ref[0] * y_ref[...].astype(F32)
    xn, h = _ln_mod(v, lng_ref[0], lnb_ref[0], sh_ref[0], sc_ref[0])
    xo_ref[...] = xn
    h_ref[...] = _pack_rows(h) if packed else h.astype(h_ref.dtype)


def _resid_ln(x, y, gate, lng, lnb, shift, scale, seg, alpha, packed):
    n, d = x.shape
    row = pl.BlockSpec((TM, d), lambda i: (i, 0))
    vec = pl.BlockSpec((1, 1, d), lambda i: (seg(i), 0, 0))
    one = pl.BlockSpec((1, 1, d), lambda i: (0, 0, 0))
    if packed:
        h_spec, h_shape = pl.BlockSpec((TM, d // 2), lambda i: (i, 0)), jax.ShapeDtypeStruct((n, d // 2), U32)
    else:
        h_spec, h_shape = row, jax.ShapeDtypeStruct((n, d), BF16)
    return pl.pallas_call(
        functools.partial(_resid_ln_kernel, alpha, packed),
        grid=(n // TM,),
        in_specs=[row, row, vec, one, one, vec, vec],
        out_specs=[row, h_spec],
        out_shape=[jax.ShapeDtypeStruct((n, d), F32), h_shape],
        compiler_params=_cp("arbitrary"),
        name="resid_ln",
    )(x, y, gate, lng, lnb, shift, scale)


def _mm_kernel(x_ref, w_ref, o_ref, wbf_ref):
    @pl.when(pl.program_id(1) == 0)
    def _():
        wbf_ref[...] = w_ref[0].astype(BF16)

    o_ref[...] = _dot(x_ref[...], wbf_ref[...]).astype(o_ref.dtype)


def _row_tile(n):
    for tm in (768, 512, 256):
        if n % tm == 0:
            return tm
    raise ValueError(f"no row tile divides {n}")


def _mm(x, w, layer, out_dtype, tn):
    n, k = x.shape
    m = w.shape[2]
    tn = min(tn, m)
    tm = _row_tile(n)
    assert m % tn == 0
    return pl.pallas_call(
        _mm_kernel,
        grid=(m // tn, n // tm),
        in_specs=[
            pl.BlockSpec((tm, k), lambda j, i: (i, 0)),
            pl.BlockSpec((1, k, tn), lambda j, i: (layer, 0, j)),
        ],
        out_specs=pl.BlockSpec((tm, tn), lambda j, i: (i, j)),
        out_shape=jax.ShapeDtypeStruct((n, m), out_dtype),
        scratch_shapes=[pltpu.VMEM((k, tn), BF16)],
        compiler_params=_cp("arbitrary", "arbitrary"),
        name="mm",
    )(x, w)


NA_HG = 2
NA_GROUP = 4
NA_UNION = NA_KH + NA_GROUP - 1


def _na_group_geometry(rows):
    cases, ids = [], []
    for g in range(rows // NA_GROUP):
        r0 = g * NA_GROUP
        u_start = int(np.clip(r0 - NA_KH // 2, 0, rows - NA_UNION))
        geo = []
        for u in range(NA_GROUP):
            r = r0 + u
            r_start = int(np.clip(r - NA_KH // 2, 0, rows - NA_KH))
            assert u_start <= r_start and r_start + NA_KH <= u_start + NA_UNION
            geo.append((r - u_start, r - r_start))
        geo = tuple(geo)
        if geo not in cases:
            cases.append(geo)
        ids.append(cases.index(geo))
    return cases, np.asarray(ids, np.int32)


def _na_bias_tables(rpb, cases):
    col = np.arange(GRID_W)
    c_start = np.clip(col - NA_KW // 2, 0, GRID_W - NA_KW)
    col_ok = (col[None, :] >= c_start[:, None]) & (col[None, :] < c_start[:, None] + NA_KW)
    dc = np.clip(col[None, :] - col[:, None], 1 - NA_KW, NA_KW - 1) + NA_KW - 1
    onehot = (dc[:, None, :] == np.arange(2 * NA_KW - 1)[None, :, None]).astype(np.float32)
    t15 = jnp.einsum("hrd,qdk->hrqk", rpb, jnp.asarray(onehot), precision=HI)
    t15 = jnp.where(col_ok[None, None], t15, NEG_INF)
    neg = jnp.full((rpb.shape[0], GRID_W, GRID_W), NEG_INF, F32)
    tabs = []
    for geo in cases:
        blocks = []
        for e, dl in geo:
            tiles = [t15[:, a - e + NA_KH - 1] if 0 <= a - (e - dl) < NA_KH else neg for a in range(NA_UNION)]
            blocks.append(jnp.concatenate(tiles, axis=-1))
        tabs.append(jnp.concatenate(blocks, axis=1))
    return jnp.stack(tabs, axis=1)


def _na_kernel(s_lat, c_ctx, hd, case_ref, q_ref, k_ref, v_ref, bias_ref, o_ref):
    rows = s_lat // GRID_W
    gq = NA_GROUP * GRID_W
    gk = NA_UNION * GRID_W
    scale = hd ** -0.5
    ctx = slice(s_lat, s_lat + c_ctx)

    def group(g, hh):
        ln = slice(hh * hd, (hh + 1) * hd)
        u_start = jnp.clip(g * NA_GROUP - NA_KH // 2, 0, rows - NA_UNION)
        q0 = pl.multiple_of(g * gq, gq)
        k0 = pl.multiple_of(u_start * GRID_W, GRID_W)
        q_g = q_ref[pl.ds(q0, gq), ln]
        s_band = _dot_nt(q_g, k_ref[pl.ds(k0, gk), ln]) * scale + bias_ref[hh, case_ref[g]]
        s_ctx = _dot_nt(q_g, k_ref[ctx, ln]) * scale
        m = jnp.maximum(jnp.max(s_band, axis=-1, keepdims=True), jnp.max(s_ctx, axis=-1, keepdims=True))
        p_band = jnp.exp(s_band - m)
        p_ctx = jnp.exp(s_ctx - m)
        den = jnp.sum(p_band, axis=-1, keepdims=True) + jnp.sum(p_ctx, axis=-1, keepdims=True)
        o = _dot(p_band.astype(BF16), v_ref[pl.ds(k0, gk), ln]) + _dot(p_ctx.astype(BF16), v_ref[ctx, ln])
        o_ref[pl.ds(q0, gq), ln] = (o / den).astype(o_ref.dtype)

    def body(g, carry):
        for hh in range(NA_HG):
            group(g, hh)
        return carry

    lax.fori_loop(0, rows // NA_GROUP, body, 0)
    for hh in range(NA_HG):
        ln = slice(hh * hd, (hh + 1) * hd)
        s = _dot_nt(q_ref[ctx, ln], k_ref[ctx, ln]) * scale
        p = jnp.exp(s - jnp.max(s, axis=-1, keepdims=True))
        o = _dot(p.astype(BF16), v_ref[ctx, ln]) / jnp.sum(p, axis=-1, keepdims=True)
        o_ref[ctx, ln] = o.astype(o_ref.dtype)


def _na_attention(qkv, rpb, batch, s_lat, c_ctx, d):
    r = s_lat + c_ctx
    hd = d // NA_HEADS
    wb = NA_HG * hd
    nhg = NA_HEADS // NA_HG
    rows = s_lat // GRID_W
    assert rows % NA_GROUP == 0 and rows >= NA_UNION
    cases, case_ids = _na_group_geometry(rows)
    bias = _na_bias_tables(rpb, cases)
    grid_spec = pltpu.PrefetchScalarGridSpec(
        num_scalar_prefetch=1,
        grid=(batch, nhg),
        in_specs=[
            pl.BlockSpec((r, wb), lambda b, g, c: (b, g)),
            pl.BlockSpec((r, wb), lambda b, g, c: (b, nhg + g)),
            pl.BlockSpec((r, wb), lambda b, g, c: (b, 2 * nhg + g)),
            pl.BlockSpec((NA_HG,) + bias.shape[1:], lambda b, g, c: (g, 0, 0, 0)),
        ],
        out_specs=pl.BlockSpec((r, wb), lambda b, g, c: (b, g)),
    )
    return pl.pallas_call(
        functools.partial(_na_kernel, s_lat, c_ctx, hd),
        grid_spec=grid_spec,
        out_shape=jax.ShapeDtypeStruct((batch * r, d), BF16),
        compiler_params=_cp("arbitrary", "arbitrary"),
        name="na_attention",
    )(jnp.asarray(case_ids), qkv, qkv, qkv, bias)


def _rope_tables(s_lat, c_ctx, dk):
    half = dk // 2
    nf = half // 2
    pos = np.arange(s_lat)
    inv = ROPE_BASE ** (-np.arange(nf, dtype=np.float32) / nf)
    ang_r = (pos // GRID_W).astype(np.float32)[:, None] * inv
    ang_c = (pos % GRID_W).astype(np.float32)[:, None] * inv
    cos = np.concatenate([np.cos(ang_r), np.cos(ang_r), np.cos(ang_c), np.cos(ang_c)], axis=1)
    sin = np.concatenate([-np.sin(ang_r), np.sin(ang_r), -np.sin(ang_c), np.sin(ang_c)], axis=1)
    cos = np.concatenate([cos, np.ones((c_ctx, dk), np.float32)], axis=0)
    sin = np.concatenate([sin, np.zeros((c_ctx, dk), np.float32)], axis=0)
    return jnp.asarray(cos, F32), jnp.asarray(sin, F32)


def _swap_quarters(u):
    parts = [pltpu.roll(u[:, i:i + LANES], LANES // 2, 1) for i in range(0, u.shape[1], LANES)]
    return jnp.concatenate(parts, axis=1)


def _split2(a):
    hi = a.astype(BF16)
    return hi, (a - hi.astype(F32)).astype(BF16)


def _gla_block(dk, dirn, hh, q_ref, k_ref, v_ref, r_ref, cos_ref, sin_ref, w2_ref, ba_ref, o_ref, st_ref):
    L = GLA_CHUNK
    tb = q_ref.shape[0]
    nch = tb // L
    dv = st_ref.shape[2]
    qk_cols = slice(hh * dk, (hh + 1) * dk)
    v_cols = slice(hh * dv, (hh + 1) * dv)
    row = lax.broadcasted_iota(jnp.int32, (tb, tb), 0)
    col = lax.broadcasted_iota(jnp.int32, (tb, tb), 1)
    tri = ((row // L) == (col // L)) & ((row >= col) if dirn == 0 else (row <= col))
    cos = cos_ref[...]
    sin = sin_ref[...]
    q = q_ref[:, qk_cols].astype(F32) * (dk ** -0.5)
    k = k_ref[:, qk_cols].astype(F32)
    q = q * cos + _swap_quarters(q) * sin
    k = k * cos + _swap_quarters(k) * sin
    v = v_ref[:, v_cols]
    r_hi, r_lo = _split2(r_ref[...])
    pre = _dot(jnp.concatenate([r_hi, r_lo, r_hi], axis=1), w2_ref[dirn, hh]) + ba_ref[dirn, hh]
    la = (jnp.minimum(pre, 0.0) - jnp.log(1.0 + jnp.exp(-jnp.abs(pre)))) / GLA_GATE_NORMALIZER
    la_hi = la.astype(BF16)
    la_mid, la_lo = _split2(la - la_hi.astype(F32))
    b3 = _dot(jnp.where(tri, 1.0, 0.0).astype(BF16), jnp.concatenate([la_hi, la_mid, la_lo], axis=1))
    b = b3[:, :dk] + b3[:, dk:2 * dk] + b3[:, 2 * dk:]
    qe, ke, qd, kd, decay = [], [], [], [], []
    for c in range(nch):
        sl = slice(c * L, (c + 1) * L)
        bc = b[sl]
        mc = bc[L // 2:L // 2 + 1]
        blc = bc[L - 1:L] if dirn == 0 else bc[0:1]
        qe.append((q[sl] * jnp.exp(bc - mc)).astype(BF16))
        ke.append((k[sl] * jnp.exp(mc - bc)).astype(BF16))
        qd.append((q[sl] * jnp.exp(bc)).astype(BF16))
        kd.append((k[sl] * jnp.exp(blc - bc)).astype(BF16))
        decay.append(jnp.exp(blc))
    attn = _dot_nt(jnp.concatenate(qe, axis=0), jnp.concatenate(ke, axis=0))
    intra = _dot(jnp.where(tri, attn, 0.0).astype(BF16), v)
    for c in (range(nch) if dirn == 0 else range(nch - 1, -1, -1)):
        sl = slice(c * L, (c + 1) * L)
        st = st_ref[dirn, hh]
        o_ref[sl, v_cols] = (intra[sl] + _dot_nt(qd[c], st.astype(BF16))).astype(o_ref.dtype)
        upd = lax.dot_general(v[sl], kd[c], (((0,), (0,)), ((), ())), preferred_element_type=F32)
        st_ref[dirn, hh] = decay[c] * st + upd


GLA_HPS = 2


def _gla_scan_kernel(dk, qf, kf, vf, rf, cosf, sinf, qb, kb, vb, rb, cosb, sinb, w2_ref, ba_ref, of_ref, ob_ref,
                     st_ref):
    @pl.when(pl.program_id(2) == 0)
    def _():
        st_ref[...] = jnp.zeros_like(st_ref)

    for hh in range(GLA_HPS):
        _gla_block(dk, 0, hh, qf, kf, vf, rf, cosf, sinf, w2_ref, ba_ref, of_ref, st_ref)
        _gla_block(dk, 1, hh, qb, kb, vb, rb, cosb, sinb, w2_ref, ba_ref, ob_ref, st_ref)


def _gla_scan(z, r, w2pad, ba, cos, sin, batch, s_lat, c_ctx, d):
    n = z.shape[0]
    hg = GLA_HEADS
    dkt = d // 2
    dk, dv = dkt // hg, d // hg
    nl = s_lat // TM
    nt = nl + 1
    assert c_ctx == TM

    blks = (lambda s: jnp.where(s == 0, nl, s - 1), lambda s: jnp.where(s == 0, nl, nl - s))
    assert hg % GLA_HPS == 0
    ng = hg // GLA_HPS
    wk, wv = GLA_HPS * dk, GLA_HPS * dv
    in_specs = []
    for blk in blks:
        in_specs += [
            pl.BlockSpec((TM, wk), lambda b, h, s, blk=blk: (b * nt + blk(s), h)),
            pl.BlockSpec((TM, wk), lambda b, h, s, blk=blk: (b * nt + blk(s), ng + h)),
            pl.BlockSpec((TM, wv), lambda b, h, s, blk=blk: (b * nt + blk(s), 2 * dkt // wv + h)),
            pl.BlockSpec((TM, LANES), lambda b, h, s, blk=blk: (b * nt + blk(s), 0)),
            pl.BlockSpec((TM, dk), lambda b, h, s, blk=blk: (blk(s), 0)),
            pl.BlockSpec((TM, dk), lambda b, h, s, blk=blk: (blk(s), 0)),
        ]
    in_specs += [
        pl.BlockSpec((2, GLA_HPS, 3 * LANES, dk), lambda b, h, s: (0, h, 0, 0)),
        pl.BlockSpec((2, GLA_HPS, 1, dk), lambda b, h, s: (0, h, 0, 0)),
    ]
    out_specs = [pl.BlockSpec((TM, wv), lambda b, h, s, blk=blk: (b * nt + blk(s), h)) for blk in blks]
    return pl.pallas_call(
        functools.partial(_gla_scan_kernel, dk),
        grid=(batch, ng, nt),
        in_specs=in_specs,
        out_specs=out_specs,
        out_shape=[jax.ShapeDtypeStruct((n, d), F32), jax.ShapeDtypeStruct((n, d), F32)],
        scratch_shapes=[pltpu.VMEM((2, GLA_HPS, dv, dk), F32)],
        compiler_params=_cp("arbitrary", "arbitrary", "arbitrary"),
        name="gla_scan",
    )(z, z, z, r, cos, sin, z, z, z, r, cos, sin, w2pad, ba)


def _gla_out_kernel(dv, of_ref, ob_ref, g_ref, nw_ref, h_ref):
    nw = nw_ref[0]
    for h in range(of_ref.shape[1] // dv):
        sl = slice(h * dv, (h + 1) * dv)
        o = of_ref[:, sl] + ob_ref[:, sl]
        o = o * lax.rsqrt(jnp.mean(o * o, axis=-1, keepdims=True) + LN_EPS) * nw
        h_ref[:, sl] = (o * _silu(g_ref[:, sl].astype(F32))).astype(h_ref.dtype)


def _gla_out(o_f, o_b, z, norm_w, layer, d):
    n = z.shape[0]
    dv = d // GLA_HEADS
    gcol = (z.shape[1] - d) // d
    row = pl.BlockSpec((TM, d), lambda i: (i, 0))
    return pl.pallas_call(
        functools.partial(_gla_out_kernel, dv),
        grid=(n // TM,),
        in_specs=[
            row, row,
            pl.BlockSpec((TM, d), lambda i: (i, gcol)),
            pl.BlockSpec((1, 1, dv), lambda i: (layer, 0, 0)),
        ],
        out_specs=row,
        out_shape=jax.ShapeDtypeStruct((n, d), BF16),
        compiler_params=_cp("arbitrary"),
        name="gla_out",
    )(o_f, o_b, z, norm_w.reshape(norm_w.shape[0], 1, dv))


def _rank_desc(v, axis_len):
    idx = lax.broadcasted_iota(jnp.int32, v.shape, 0)
    rank = jnp.zeros(v.shape, jnp.int32)
    for j in range(axis_len):
        vj = v[j:j + 1]
        rank = rank + jnp.where((vj > v) | ((vj == v) & (j < idx)), 1, 0)
    return rank


def _router_kernel(n_e, x_ref, sh_ref, sc_ref, wr_ref, eb_ref, eid_ref, rnk_ref, gw_ref, cnt_ref, run_ref):
    @pl.when(pl.program_id(0) == 0)
    def _():
        run_ref[...] = jnp.zeros_like(run_ref)

    tm = x_ref.shape[0]
    gsz = n_e // N_GROUPS
    h = x_ref[...] * (1.0 + sc_ref[0]) + sh_ref[0]
    logits = lax.dot_general(wr_ref[0], h, (((1,), (1,)), ((), ())), precision=HI, preferred_element_type=F32)
    scores = _sigmoid(logits)
    biased = scores + eb_ref[0]
    grp = biased.reshape(N_GROUPS, gsz, tm)
    m1 = jnp.max(grp, axis=1, keepdims=True)
    gi = lax.broadcasted_iota(jnp.int32, grp.shape, 1)
    first = jnp.min(jnp.where(grp == m1, gi, gsz), axis=1, keepdims=True)
    m2 = jnp.max(jnp.where(gi == first, -jnp.inf, grp), axis=1, keepdims=True)
    g_sel = _rank_desc(m1 + m2, N_GROUPS) < TOPK_GROUPS
    e_sel = jnp.broadcast_to(g_sel, (N_GROUPS, gsz, tm)).reshape(n_e, tm)
    masked = jnp.where(e_sel, biased, NEG_INF)
    eidx_i = lax.broadcasted_iota(jnp.int32, (n_e, tm), 0)
    firsts, hits = [], []
    sel_f = jnp.zeros((n_e, tm), F32)
    for _ in range(TOP_K):
        mx = jnp.max(masked, axis=0, keepdims=True)
        first = jnp.min(jnp.where(masked == mx, eidx_i, n_e), axis=0, keepdims=True)
        hit = eidx_i == first
        firsts.append(first)
        hits.append(hit)
        sel_f = sel_f + jnp.where(hit, 1.0, 0.0)
        masked = jnp.where(hit, -jnp.inf, masked)
    w = scores * sel_f
    gates = w / jnp.sum(w, axis=0, keepdims=True) * ROUTED_SCALE
    t0 = lax.broadcasted_iota(jnp.int32, (tm, tm), 0)
    t1 = lax.broadcasted_iota(jnp.int32, (tm, tm), 1)
    before = _dot(sel_f.astype(BF16), jnp.where(t0 < t1, 1.0, 0.0).astype(BF16))
    pos = run_ref[:, 0:1] + before
    run_ref[...] = run_ref[...] + jnp.sum(sel_f, axis=1, keepdims=True)
    cnt_ref[...] = run_ref[...].astype(jnp.int32)
    for j in range(TOP_K):
        eid_ref[j:j + 1, :] = firsts[j]
        rnk_ref[j:j + 1, :] = jnp.sum(jnp.where(hits[j], pos, 0.0), axis=0, keepdims=True).astype(jnp.int32)
        gw_ref[j:j + 1, :] = jnp.sum(jnp.where(hits[j], gates, 0.0), axis=0, keepdims=True)


def _router(x, shift, scale, w_router_t, e_bias, layer, seg):
    n, d = x.shape
    n_e = w_router_t.shape[1]
    vec = pl.BlockSpec((1, 1, d), lambda i: (seg(i), 0, 0))
    sel = pl.BlockSpec((TOP_K, TM), lambda i: (0, i))
    return pl.pallas_call(
        functools.partial(_router_kernel, n_e),
        grid=(n // TM,),
        in_specs=[
            pl.BlockSpec((TM, d), lambda i: (i, 0)),
            vec, vec,
            pl.BlockSpec((1, n_e, d), lambda i: (layer, 0, 0)),
            pl.BlockSpec((1, n_e, 1), lambda i: (layer, 0, 0)),
        ],
        out_specs=[sel, sel, sel, pl.BlockSpec((n_e, LANES), lambda i: (0, 0))],
        out_shape=[
            jax.ShapeDtypeStruct((TOP_K, n), jnp.int32),
            jax.ShapeDtypeStruct((TOP_K, n), jnp.int32),
            jax.ShapeDtypeStruct((TOP_K, n), F32),
            jax.ShapeDtypeStruct((n_e, LANES), jnp.int32),
        ],
        scratch_shapes=[pltpu.VMEM((n_e, LANES), F32)],
        compiler_params=_cp("arbitrary"),
        name="router",
    )(x, shift, scale, w_router_t, e_bias)


ROW_UNROLL = 8
GROUP_UNROLL = 8


def _gather_rows(idx_ref, src_ref, dst_ref, sem, n_rows):
    def group(g, c):
        for v in range(GROUP_UNROLL):
            for u in range(ROW_UNROLL):
                t = idx_ref[0, 0, (g * GROUP_UNROLL + v) * ROW_UNROLL + u]
                src = src_ref.at[lax.shift_right_logical(t, 3), pl.ds(t & 7, 1)]
                pltpu.make_async_copy(src, dst_ref.at[g * GROUP_UNROLL + v, pl.ds(u, 1)], sem).start(priority=u % 2)
        return c

    lax.fori_loop(0, n_rows // (ROW_UNROLL * GROUP_UNROLL), group, 0)


def _wait_rows(src_ref, dst_ref, sem, n_rows):
    pltpu.make_async_copy(src_ref.at[pl.ds(0, n_rows // ROW_UNROLL)], dst_ref, sem).wait()


GATHER_AHEAD = 3


def _request_rows(step, n_valid, first_refs, ahead_ref, src_ref, buf, sem, n_rows):
    for a, ref in enumerate(first_refs):
        @pl.when((step == 0) & (n_valid > a))
        def _(a=a, ref=ref):
            _gather_rows(ref, src_ref, buf.at[a], sem.at[a], n_rows)

    @pl.when(step + GATHER_AHEAD < n_valid)
    def _():
        s = lax.rem(step + GATHER_AHEAD, GATHER_AHEAD + 1)
        _gather_rows(ahead_ref, src_ref, buf.at[s], sem.at[s], n_rows)


def _expert_kernel(te_ref, nt_ref, *refs):
    first_refs = refs[:GATHER_AHEAD]
    tokn_ref, hp_ref, wg_ref, wu_ref, wd_ref, y_ref, xbuf, wgb, wub, wdb, sem = refs[GATHER_AHEAD:]
    j = pl.program_id(0)
    nt = nt_ref[0]
    tg = xbuf.shape[1] * xbuf.shape[2]
    slot = lax.rem(j, GATHER_AHEAD + 1)
    _request_rows(j, nt, first_refs, tokn_ref, hp_ref, xbuf, sem, tg)

    @pl.when(j < nt)
    def _():
        @pl.when((j == 0) | (te_ref[j] != te_ref[jnp.maximum(j - 1, 0)]))
        def _():
            wgb[...] = wg_ref[0, 0].astype(BF16)
            wub[...] = wu_ref[0, 0].astype(BF16)
            wdb[...] = wd_ref[0, 0].astype(BF16)

        _wait_rows(hp_ref, xbuf.at[slot], sem.at[slot], tg)
        lo, hi = _unpack_rows(xbuf[slot].reshape(tg, xbuf.shape[3]))
        x = jnp.concatenate([lo.astype(BF16), hi.astype(BF16)], axis=1)
        hid = _silu(_dot_nt(x, wgb[...])) * _dot_nt(x, wub[...])
        y_ref[...] = _pack_rows(_dot(hid.astype(BF16), wdb[...]))

    @pl.when(j >= nt)
    def _():
        y_ref[...] = jnp.zeros_like(y_ref)


def _experts(hp, tok, tile_eid, n_tiles_used, w_gate_t, w_up_t, w_down, layer):
    n, dh = hp.shape
    d = 2 * dh
    f = w_gate_t.shape[2]
    n_tiles = tok.shape[0]
    assert n % ROW_UNROLL == 0 and TG % ROW_UNROLL == 0
    grid_spec = pltpu.PrefetchScalarGridSpec(
        num_scalar_prefetch=2,
        grid=(n_tiles,),
        in_specs=[
            pl.BlockSpec((1, 1, TG), lambda j, te, nt, a=a: (a, 0, 0), memory_space=pltpu.SMEM)
            for a in range(GATHER_AHEAD)
        ] + [
            pl.BlockSpec((1, 1, TG), lambda j, te, nt: (jnp.minimum(j + GATHER_AHEAD, n_tiles - 1), 0, 0),
                         memory_space=pltpu.SMEM),
            pl.BlockSpec(memory_space=pl.ANY),
            pl.BlockSpec((1, 1, f, d), lambda j, te, nt: (layer, te[j], 0, 0)),
            pl.BlockSpec((1, 1, f, d), lambda j, te, nt: (layer, te[j], 0, 0)),
            pl.BlockSpec((1, 1, f, d), lambda j, te, nt: (layer, te[j], 0, 0)),
        ],
        out_specs=pl.BlockSpec((TG, dh), lambda j, te, nt: (j, 0)),
        scratch_shapes=[
            pltpu.VMEM((GATHER_AHEAD + 1, TG // ROW_UNROLL, ROW_UNROLL, dh), U32),
            pltpu.VMEM((f, d), BF16),
            pltpu.VMEM((f, d), BF16),
            pltpu.VMEM((f, d), BF16),
            pltpu.SemaphoreType.DMA((GATHER_AHEAD + 1,)),
        ],
    )
    return pl.pallas_call(
        _expert_kernel,
        grid_spec=grid_spec,
        out_shape=jax.ShapeDtypeStruct((n_tiles * TG, dh), U32),
        compiler_params=_cp("arbitrary"),
        name="experts",
    )(tile_eid, n_tiles_used, *([tok] * (GATHER_AHEAD + 1)), hp.reshape(n // ROW_UNROLL, ROW_UNROLL, dh),
      w_gate_t, w_up_t, w_down)


def _shared_kernel(hp_ref, sg_ref, su_ref, sd_ref, y_ref, sgb, sub, sdb):
    @pl.when(pl.program_id(0) == 0)
    def _():
        sgb[...] = sg_ref[0].astype(BF16)
        sub[...] = su_ref[0].astype(BF16)
        sdb[...] = sd_ref[0].astype(BF16)

    lo, hi = _unpack_rows(hp_ref[...])
    x = jnp.concatenate([lo.astype(BF16), hi.astype(BF16)], axis=1)
    hid = _silu(_dot_nt(x, sgb[...])) * _dot_nt(x, sub[...])
    y_ref[...] = _dot(hid.astype(BF16), sdb[...]).astype(y_ref.dtype)


def _shared_expert(hp, sg_t, su_t, sd, layer):
    n, dh = hp.shape
    d = 2 * dh
    f = sg_t.shape[1]
    tm = _row_tile(n)
    wspec = pl.BlockSpec((1, f, d), lambda i: (layer, 0, 0))
    return pl.pallas_call(
        _shared_kernel,
        grid=(n // tm,),
        in_specs=[pl.BlockSpec((tm, dh), lambda i: (i, 0)), wspec, wspec, wspec],
        out_specs=pl.BlockSpec((tm, d), lambda i: (i, 0)),
        out_shape=jax.ShapeDtypeStruct((n, d), BF16),
        scratch_shapes=[pltpu.VMEM((f, d), BF16), pltpu.VMEM((f, d), BF16), pltpu.VMEM((f, d), BF16)],
        compiler_params=_cp("arbitrary"),
        name="shared_expert",
    )(hp, sg_t, su_t, sd)


TMC = 128


def _combine_kernel(alpha, *refs):
    first_refs = refs[:GATHER_AHEAD]
    (destn_ref, ys_ref, ysh_ref, gw_ref, x_ref, g_ref, lng_ref, lnb_ref, sh_ref, sc_ref, xo_ref, h_ref, buf,
     sem) = refs[GATHER_AHEAD:]
    tm, d = x_ref.shape
    dh = d // 2
    i = pl.program_id(0)
    slot = lax.rem(i, GATHER_AHEAD + 1)
    n_rows = TOP_K * tm
    _request_rows(i, pl.num_programs(0), first_refs, destn_ref, ys_ref, buf, sem, n_rows)
    _wait_rows(ys_ref, buf.at[slot], sem.at[slot], n_rows)
    ysh = ysh_ref[...].astype(F32)
    y_lo, y_hi = ysh[:, :dh], ysh[:, dh:]
    tg = tm // ROW_UNROLL
    for kk in range(TOP_K):
        lo, hi = _unpack_rows(buf[slot, kk * tg:(kk + 1) * tg].reshape(tm, dh))
        w = gw_ref[:, kk:kk + 1]
        y_lo = y_lo + w * lo
        y_hi = y_hi + w * hi
    y = jnp.concatenate([y_lo, y_hi], axis=1)
    v = alpha * x_ref[...] + g_ref[0] * y
    xn, h = _ln_mod(v, lng_ref[0], lnb_ref[0], sh_ref[0], sc_ref[0])
    xo_ref[...] = xn
    h_ref[...] = h.astype(h_ref.dtype)


def _combine(dest, ys, ysh, gw, x, gate, lng, lnb, shift, scale, seg, alpha):
    n, d = x.shape
    row = pl.BlockSpec((TMC, d), lambda i: (i, 0))
    vec = pl.BlockSpec((1, 1, d), lambda i: (seg(i), 0, 0))
    one = pl.BlockSpec((1, 1, d), lambda i: (0, 0, 0))
    n_tiles = n // TMC
    return pl.pallas_call(
        functools.partial(_combine_kernel, alpha),
        grid=(n_tiles,),
        in_specs=[
            pl.BlockSpec((1, 1, TOP_K * TMC), lambda i, a=a: (a, 0, 0), memory_space=pltpu.SMEM)
            for a in range(GATHER_AHEAD)
        ] + [
            pl.BlockSpec((1, 1, TOP_K * TMC), lambda i: (jnp.minimum(i + GATHER_AHEAD, n_tiles - 1), 0, 0),
                         memory_space=pltpu.SMEM),
            pl.BlockSpec(memory_space=pl.ANY),
            row,
            pl.BlockSpec((TMC, TOP_K), lambda i: (i, 0)),
            row, vec, one, one, vec, vec,
        ],
        out_specs=[row, row],
        out_shape=[jax.ShapeDtypeStruct((n, d), F32), jax.ShapeDtypeStruct((n, d), BF16)],
        scratch_shapes=[pltpu.VMEM((GATHER_AHEAD + 1, TOP_K * TMC // ROW_UNROLL, ROW_UNROLL, d // 2), U32),
                        pltpu.SemaphoreType.DMA((GATHER_AHEAD + 1,))],
        compiler_params=_cp("arbitrary"),
        name="moe_combine",
    )(*([dest] * (GATHER_AHEAD + 1)), ys.reshape(ys.shape[0] // ROW_UNROLL, ROW_UNROLL, d // 2), ysh, gw, x, gate, lng, lnb, shift, scale)


def _moe(x, hp, mods_f, w_router_t, e_bias, w_gate, w_up, w_down, sg, su, sd, gate, lng, lnb, nshift, nscale,
         layer, seg, seg_c, alpha):
    n, d = x.shape
    n_e = w_router_t.shape[1]
    shift_f, scale_f = mods_f
    eid, rnk, gw, cnt = _router(x, shift_f, scale_f, w_router_t, e_bias, layer, seg)
    counts = cnt[:, 0]
    tiles_e = (counts + TG - 1) // TG
    tile_end = jnp.cumsum(tiles_e)
    n_tiles = (n * TOP_K) // TG + n_e
    tile_ids = jnp.arange(n_tiles, dtype=jnp.int32)
    tile_eid = jnp.minimum(jnp.sum((tile_end[None, :] <= tile_ids[:, None]).astype(jnp.int32), axis=1), n_e - 1)
    row_off = (tile_end - tiles_e) * TG
    e_ids = jnp.arange(n_e, dtype=jnp.int32)
    dest = jnp.sum(jnp.where(eid[..., None] == e_ids, row_off, 0), axis=-1) + rnk
    tok = jnp.zeros((n_tiles * TG,), jnp.int32).at[dest.reshape(-1)].set(
        jnp.broadcast_to(jnp.arange(n, dtype=jnp.int32), (TOP_K, n)).reshape(-1),
        unique_indices=True, mode="promise_in_bounds")
    ys = _experts(hp, tok.reshape(n_tiles, 1, TG), tile_eid, tile_end[-1:].astype(jnp.int32), w_gate, w_up, w_down,
                  layer)
    ysh = _shared_expert(hp, sg, su, sd, layer)
    dest_t = dest.reshape(TOP_K, n // TMC, TMC).transpose(1, 0, 2).reshape(n // TMC, 1, TOP_K * TMC)
    return _combine(dest_t, ys, ysh, gw.T, x, gate, lng, lnb, nshift, nscale, seg_c, alpha)


def kernel(x, c, ctx, c_ctx, ada_w1, ada_w2, ada_b, ln_g, ln_b, na_w_qkv, na_w_o, na_rpb, gla_w_in, gla_w_a1,
           gla_w_a2, gla_b_a, gla_norm, gla_w_o, moe_router, moe_bias, moe_w_gate, moe_w_up, moe_w_down,
           sh_w_gate, sh_w_up, sh_w_down):
    batch, s_lat, d = x.shape
    c_ctx_len = ctx.shape[1]
    depth = ada_w1.shape[0]
    alpha = (2 * depth) ** 0.25
    assert s_lat % TM == 0 and c_ctx_len == TM and s_lat % GRID_W == 0
    r = s_lat + c_ctx_len
    n = batch * r
    nl, nt = s_lat // TM, r // TM
    seg = _seg_fn(nt, nl)
    seg_c = _seg_fn(nt * (TM // TMC), nl * (TM // TMC))

    xa = jnp.concatenate([x, ctx], axis=1).reshape(n, d)
    cond = jnp.stack([c, jnp.broadcast_to(c_ctx, c.shape)], axis=1).reshape(2 * batch, d)
    rows = -(-2 * batch // 8) * 8
    cond = jnp.pad(cond, ((0, rows - 2 * batch), (0, 0)))
    mods = _adaln(cond, ada_w1, ada_w2, ada_b).reshape(depth, rows, 6, d).transpose(0, 2, 1, 3)
    mods = mods.reshape(depth, 6, rows, 1, d)
    lng = ln_g.reshape(depth, 2, 1, 1, d)
    lnb = ln_b.reshape(depth, 2, 1, 1, d)

    hd_gla = d // 2 // GLA_HEADS
    cos, sin = _rope_tables(s_lat, c_ctx_len, hd_gla)
    n_e = moe_router.shape[2]
    w_router_t = jnp.swapaxes(moe_router, 1, 2)
    e_bias = moe_bias.reshape(depth, n_e, 1)
    w_gate_t, w_up_t = jnp.swapaxes(moe_w_gate, 2, 3), jnp.swapaxes(moe_w_up, 2, 3)
    sg_t, su_t = jnp.swapaxes(sh_w_gate, 1, 2), jnp.swapaxes(sh_w_up, 1, 2)
    rank_g = gla_w_a1.shape[3]
    assert 2 * rank_g <= LANES

    h = _modulate(xa, mods[0, 0], mods[0, 1], seg)
    for i in range(depth):
        last = i == depth - 1
        m = mods[i]
        j = i // 2
        if i % 2 == 0:
            qkv = _mm(h, na_w_qkv, j, BF16, 512)
            o = _na_attention(qkv, na_rpb[j], batch, s_lat, c_ctx_len, d)
            y = _mm(o, na_w_o, j, F32, 512)
        else:
            z = _mm(h, gla_w_in, j, BF16, 512)
            w_a1 = jnp.concatenate([gla_w_a1[j, 0], gla_w_a1[j, 1]], axis=1)
            w_a1 = jnp.pad(w_a1, ((0, 0), (0, LANES - 2 * rank_g)))[None]
            rr = _mm(h, w_a1, 0, F32, LANES)
            w2 = gla_w_a2[j].reshape(2, rank_g, GLA_HEADS, hd_gla).transpose(0, 2, 1, 3)
            w2pad = jnp.zeros((2, GLA_HEADS, LANES, hd_gla), F32)
            w2pad = w2pad.at[0, :, :rank_g].set(w2[0]).at[1, :, rank_g:2 * rank_g].set(w2[1])
            w2_hi = w2pad.astype(BF16)
            w2_lo = (w2pad - w2_hi.astype(F32)).astype(BF16)
            w2cat = jnp.concatenate([w2_hi, w2_hi, w2_lo], axis=2)
            ba = gla_b_a[j].reshape(2, GLA_HEADS, 1, hd_gla)
            o_f, o_b = _gla_scan(z, rr, w2cat, ba, cos, sin, batch, s_lat, c_ctx_len, d)
            og = _gla_out(o_f, o_b, z, gla_norm, j, d)
            y = _mm(og, gla_w_o, j, F32, 512)
        x1, hp = _resid_ln(xa, y, m[2], lng[i, 0], lnb[i, 0], m[3], m[4], seg, alpha, packed=True)
        nxt = mods[i + 1] if not last else m
        xa, h = _moe(x1, hp, (m[3], m[4]), w_router_t, e_bias, w_gate_t, w_up_t, moe_w_down,
                     sg_t, su_t, sh_w_down, m[5], lng[i, 1], lnb[i, 1], nxt[0], nxt[1],
                     i, seg, seg_c, alpha)
    return xa.reshape(batch, r, d)[:, :s_lat]
```

```python
import functools

import jax
import jax.numpy as jnp
import numpy as np
from jax import lax
from jax.experimental import pallas as pl
from jax.experimental.pallas import tpu as pltpu

GRID_W = 64
NA_HEADS = 32
NA_KH = 8
NA_KW = 16
GLA_HEADS = 8
GLA_GATE_NORMALIZER = 16.0
GLA_CHUNK = 64
ROPE_BASE = 10000.0
N_GROUPS = 8
TOPK_GROUPS = 4
TOP_K = 8
ROUTED_SCALE = 2.5
LN_EPS = 1e-6
NEG_INF = -1e30

LANES = 128
TM = 256
TG = 512
VMEM_LIMIT = 56 * 1024 * 1024

F32 = jnp.float32
BF16 = jnp.bfloat16
U32 = jnp.uint32
HI = lax.Precision.HIGHEST


def _cp(*sem):
    return pltpu.CompilerParams(dimension_semantics=sem, vmem_limit_bytes=VMEM_LIMIT)


def _sigmoid(x):
    return 1.0 / (1.0 + jnp.exp(-x))


def _silu(x):
    return x * _sigmoid(x)


def _dot(a, b):
    return jnp.dot(a, b, preferred_element_type=F32)


def _dot_nt(a, b):
    return lax.dot_general(a, b, (((1,), (1,)), ((), ())), preferred_element_type=F32)


def _pack_rows(h):
    w = h.shape[1] // 2
    lo = lax.bitcast_convert_type(h[:, :w].astype(BF16).astype(F32), U32)
    hi = lax.bitcast_convert_type(h[:, w:].astype(BF16).astype(F32), U32)
    return (hi & jnp.uint32(0xFFFF0000)) | (lo >> 16)


def _unpack_rows(p):
    lo = lax.bitcast_convert_type(p << 16, F32)
    hi = lax.bitcast_convert_type(p & jnp.uint32(0xFFFF0000), F32)
    return lo, hi


def _ln_mod(v, lng, lnb, shift, scale):
    mu = jnp.mean(v, axis=-1, keepdims=True)
    d = v - mu
    var = jnp.mean(d * d, axis=-1, keepdims=True)
    xn = d * lax.rsqrt(var + LN_EPS) * lng + lnb
    return xn, xn * (1.0 + scale) + shift


def _adaln_kernel(c_ref, w1_ref, w2_ref, b_ref, o_ref):
    a = _silu(c_ref[...])
    t = jnp.dot(a, w1_ref[0], precision=HI, preferred_element_type=F32)
    o_ref[0] = jnp.dot(t, w2_ref[0], precision=HI, preferred_element_type=F32) + b_ref[0]


def _adaln(cond, w1, w2, b):
    depth, d, r = w1.shape
    n6 = w2.shape[2]
    tn = min(n6, 2048)
    rows = cond.shape[0]
    return pl.pallas_call(
        _adaln_kernel,
        grid=(depth, n6 // tn),
        in_specs=[
            pl.BlockSpec((rows, d), lambda l, j: (0, 0)),
            pl.BlockSpec((1, d, r), lambda l, j: (l, 0, 0)),
            pl.BlockSpec((1, r, tn), lambda l, j: (l, 0, j)),
            pl.BlockSpec((1, 1, tn), lambda l, j: (l, 0, j)),
        ],
        out_specs=pl.BlockSpec((1, rows, tn), lambda l, j: (l, 0, j)),
        out_shape=jax.ShapeDtypeStruct((depth, rows, n6), F32),
        compiler_params=_cp("arbitrary", "arbitrary"),
        name="adaln",
    )(cond, w1, w2, b.reshape(depth, 1, n6))


def _seg_fn(nt, nl):
    return lambda i: 2 * (i // nt) + (i % nt) // nl


def _modulate_kernel(x_ref, sh_ref, sc_ref, h_ref):
    h_ref[...] = (x_ref[...] * (1.0 + sc_ref[0]) + sh_ref[0]).astype(h_ref.dtype)


def _modulate(x, shift, scale, seg):
    n, d = x.shape
    row = pl.BlockSpec((TM, d), lambda i: (i, 0))
    vec = pl.BlockSpec((1, 1, d), lambda i: (seg(i), 0, 0))
    return pl.pallas_call(
        _modulate_kernel,
        grid=(n // TM,),
        in_specs=[row, vec, vec],
        out_specs=row,
        out_shape=jax.ShapeDtypeStruct((n, d), BF16),
        compiler_params=_cp("arbitrary"),
        name="modulate",
    )(x, shift, scale)


def _resid_ln_kernel(alpha, packed, x_ref, y_ref, g_ref, lng_ref, lnb_ref, sh_ref, sc_ref, xo_ref, h_ref):
    v = alpha * x_ref[...] + g_ref[0] * y_ref[...].astype(F32)
    xn, h = _ln_mod(v, lng_ref[0], lnb_ref[0], sh_ref[0], sc_ref[0])
    xo_ref[...] = xn
    h_ref[...] = _pack_rows(h) if packed else h.astype(h_ref.dtype)


def _resid_ln(x, y, gate, lng, lnb, shift, scale, seg, alpha, packed):
    n, d = x.shape
    row = pl.BlockSpec((TM, d), lambda i: (i, 0))
    vec = pl.BlockSpec((1, 1, d), lambda i: (seg(i), 0, 0))
    one = pl.BlockSpec((1, 1, d), lambda i: (0, 0, 0))
    if packed:
        h_spec, h_shape = pl.BlockSpec((TM, d // 2), lambda i: (i, 0)), jax.ShapeDtypeStruct((n, d // 2), U32)
    else:
        h_spec, h_shape = row, jax.ShapeDtypeStruct((n, d), BF16)
    return pl.pallas_call(
        functools.partial(_resid_ln_kernel, alpha, packed),
        grid=(n // TM,),
        in_specs=[row, row, vec, one, one, vec, vec],
        out_specs=[row, h_spec],
        out_shape=[jax.ShapeDtypeStruct((n, d), F32), h_shape],
        compiler_params=_cp("arbitrary"),
        name="resid_ln",
    )(x, y, gate, lng, lnb, shift, scale)


def _mm_kernel(x_ref, w_ref, o_ref, wbf_ref):
    @pl.when(pl.program_id(1) == 0)
    def _():
        wbf_ref[...] = w_ref[0].astype(BF16)

    o_ref[...] = _dot(x_ref[...], wbf_ref[...]).astype(o_ref.dtype)


def _row_tile(n):
    for tm in (768, 512, 256):
        if n % tm == 0:
            return tm
    raise ValueError(f"no row tile divides {n}")


def _mm(x, w, layer, out_dtype, tn):
    n, k = x.shape
    m = w.shape[2]
    tn = min(tn, m)
    tm = _row_tile(n)
    assert m % tn == 0
    return pl.pallas_call(
        _mm_kernel,
        grid=(m // tn, n // tm),
        in_specs=[
            pl.BlockSpec((tm, k), lambda j, i: (i, 0)),
            pl.BlockSpec((1, k, tn), lambda j, i: (layer, 0, j)),
        ],
        out_specs=pl.BlockSpec((tm, tn), lambda j, i: (i, j)),
        out_shape=jax.ShapeDtypeStruct((n, m), out_dtype),
        scratch_shapes=[pltpu.VMEM((k, tn), BF16)],
        compiler_params=_cp("arbitrary", "arbitrary"),
        name="mm",
    )(x, w)


NA_HG = 2
NA_GROUP = 4
NA_UNION = NA_KH + NA_GROUP - 1


def _na_group_geometry(rows):
    cases, ids = [], []
    for g in range(rows // NA_GROUP):
        r0 = g * NA_GROUP
        u_start = int(np.clip(r0 - NA_KH // 2, 0, rows - NA_UNION))
        geo = []
        for u in range(NA_GROUP):
            r = r0 + u
            r_start = int(np.clip(r - NA_KH // 2, 0, rows - NA_KH))
            assert u_start <= r_start and r_start + NA_KH <= u_start + NA_UNION
            geo.append((r - u_start, r - r_start))
        geo = tuple(geo)
        if geo not in cases:
            cases.append(geo)
        ids.append(cases.index(geo))
    return cases, np.asarray(ids, np.int32)


def _na_bias_tables(rpb, cases):
    col = np.arange(GRID_W)
    c_start = np.clip(col - NA_KW // 2, 0, GRID_W - NA_KW)
    col_ok = (col[None, :] >= c_start[:, None]) & (col[None, :] < c_start[:, None] + NA_KW)
    dc = np.clip(col[None, :] - col[:, None], 1 - NA_KW, NA_KW - 1) + NA_KW - 1
    onehot = (dc[:, None, :] == np.arange(2 * NA_KW - 1)[None, :, None]).astype(np.float32)
    t15 = jnp.einsum("hrd,qdk->hrqk", rpb, jnp.asarray(onehot), precision=HI)
    t15 = jnp.where(col_ok[None, None], t15, NEG_INF)
    neg = jnp.full((rpb.shape[0], GRID_W, GRID_W), NEG_INF, F32)
    tabs = []
    for geo in cases:
        blocks = []
        for e, dl in geo:
            tiles = [t15[:, a - e + NA_KH - 1] if 0 <= a - (e - dl) < NA_KH else neg for a in range(NA_UNION)]
            blocks.append(jnp.concatenate(tiles, axis=-1))
        tabs.append(jnp.concatenate(blocks, axis=1))
    return jnp.stack(tabs, axis=1)


def _na_kernel(s_lat, c_ctx, hd, case_ref, q_ref, k_ref, v_ref, bias_ref, o_ref):
    rows = s_lat // GRID_W
    gq = NA_GROUP * GRID_W
    gk = NA_UNION * GRID_W
    scale = hd ** -0.5
    ctx = slice(s_lat, s_lat + c_ctx)

    def group(g, hh):
        ln = slice(hh * hd, (hh + 1) * hd)
        u_start = jnp.clip(g * NA_GROUP - NA_KH // 2, 0, rows - NA_UNION)
        q0 = pl.multiple_of(g * gq, gq)
        k0 = pl.multiple_of(u_start * GRID_W, GRID_W)
        q_g = q_ref[pl.ds(q0, gq), ln]
        s_band = _dot_nt(q_g, k_ref[pl.ds(k0, gk), ln]) * scale + bias_ref[hh, case_ref[g]]
        s_ctx = _dot_nt(q_g, k_ref[ctx, ln]) * scale
        m = jnp.maximum(jnp.max(s_band, axis=-1, keepdims=True), jnp.max(s_ctx, axis=-1, keepdims=True))
        p_band = jnp.exp(s_band - m)
        p_ctx = jnp.exp(s_ctx - m)
        den = jnp.sum(p_band, axis=-1, keepdims=True) + jnp.sum(p_ctx, axis=-1, keepdims=True)
        o = _dot(p_band.astype(BF16), v_ref[pl.ds(k0, gk), ln]) + _dot(p_ctx.astype(BF16), v_ref[ctx, ln])
        o_ref[pl.ds(q0, gq), ln] = (o / den).astype(o_ref.dtype)

    def body(g, carry):
        for hh in range(NA_HG):
            group(g, hh)
        return carry

    lax.fori_loop(0, rows // NA_GROUP, body, 0)
    for hh in range(NA_HG):
        ln = slice(hh * hd, (hh + 1) * hd)
        s = _dot_nt(q_ref[ctx, ln], k_ref[ctx, ln]) * scale
        p = jnp.exp(s - jnp.max(s, axis=-1, keepdims=True))
        o = _dot(p.astype(BF16), v_ref[ctx, ln]) / jnp.sum(p, axis=-1, keepdims=True)
        o_ref[ctx, ln] = o.astype(o_ref.dtype)


def _na_attention(qkv, rpb, batch, s_lat, c_ctx, d):
    r = s_lat + c_ctx
    hd = d // NA_HEADS
    wb = NA_HG * hd
    nhg = NA_HEADS // NA_HG
    rows = s_lat // GRID_W
    assert rows % NA_GROUP == 0 and rows >= NA_UNION
    cases, case_ids = _na_group_geometry(rows)
    bias = _na_bias_tables(rpb, cases)
    grid_spec = pltpu.PrefetchScalarGridSpec(
        num_scalar_prefetch=1,
        grid=(batch, nhg),
        in_specs=[
            pl.BlockSpec((r, wb), lambda b, g, c: (b, g)),
            pl.BlockSpec((r, wb), lambda b, g, c: (b, nhg + g)),
            pl.BlockSpec((r, wb), lambda b, g, c: (b, 2 * nhg + g)),
            pl.BlockSpec((NA_HG,) + bias.shape[1:], lambda b, g, c: (g, 0, 0, 0)),
        ],
        out_specs=pl.BlockSpec((r, wb), lambda b, g, c: (b, g)),
    )
    return pl.pallas_call(
        functools.partial(_na_kernel, s_lat, c_ctx, hd),
        grid_spec=grid_spec,
        out_shape=jax.ShapeDtypeStruct((batch * r, d), BF16),
        compiler_params=_cp("arbitrary", "arbitrary"),
        name="na_attention",
    )(jnp.asarray(case_ids), qkv, qkv, qkv, bias)


def _rope_tables(s_lat, c_ctx, dk):
    half = dk // 2
    nf = half // 2
    pos = np.arange(s_lat)
    inv = ROPE_BASE ** (-np.arange(nf, dtype=np.float32) / nf)
    ang_r = (pos // GRID_W).astype(np.float32)[:, None] * inv
    ang_c = (pos % GRID_W).astype(np.float32)[:, None] * inv
    cos = np.concatenate([np.cos(ang_r), np.cos(ang_r), np.cos(ang_c), np.cos(ang_c)], axis=1)
    sin = np.concatenate([-np.sin(ang_r), np.sin(ang_r), -np.sin(ang_c), np.sin(ang_c)], axis=1)
    cos = np.concatenate([cos, np.ones((c_ctx, dk), np.float32)], axis=0)
    sin = np.concatenate([sin, np.zeros((c_ctx, dk), np.float32)], axis=0)
    return jnp.asarray(cos, F32), jnp.asarray(sin, F32)


def _swap_quarters(u):
    parts = [pltpu.roll(u[:, i:i + LANES], LANES // 2, 1) for i in range(0, u.shape[1], LANES)]
    return jnp.concatenate(parts, axis=1)


def _split2(a):
    hi = a.astype(BF16)
    return hi, (a - hi.astype(F32)).astype(BF16)


def _gla_block(dk, dirn, hh, q_ref, k_ref, v_ref, r_ref, cos_ref, sin_ref, w2_ref, ba_ref, o_ref, st_ref):
    L = GLA_CHUNK
    tb = q_ref.shape[0]
    nch = tb // L
    dv = st_ref.shape[2]
    qk_cols = slice(hh * dk, (hh + 1) * dk)
    v_cols = slice(hh * dv, (hh + 1) * dv)
    row = lax.broadcasted_iota(jnp.int32, (tb, tb), 0)
    col = lax.broadcasted_iota(jnp.int32, (tb, tb), 1)
    tri = ((row // L) == (col // L)) & ((row >= col) if dirn == 0 else (row <= col))
    cos = cos_ref[...]
    sin = sin_ref[...]
    q = q_ref[:, qk_cols].astype(F32) * (dk ** -0.5)
    k = k_ref[:, qk_cols].astype(F32)
    q = q * cos + _swap_quarters(q) * sin
    k = k * cos + _swap_quarters(k) * sin
    v = v_ref[:, v_cols]
    r_hi, r_lo = _split2(r_ref[...])
    pre = _dot(jnp.concatenate([r_hi, r_lo, r_hi], axis=1), w2_ref[dirn, hh]) + ba_ref[dirn, hh]
    la = (jnp.minimum(pre, 0.0) - jnp.log(1.0 + jnp.exp(-jnp.abs(pre)))) / GLA_GATE_NORMALIZER
    la_hi = la.astype(BF16)
    la_mid, la_lo = _split2(la - la_hi.astype(F32))
    b3 = _dot(jnp.where(tri, 1.0, 0.0).astype(BF16), jnp.concatenate([la_hi, la_mid, la_lo], axis=1))
    b = b3[:, :dk] + b3[:, dk:2 * dk] + b3[:, 2 * dk:]
    qe, ke, qd, kd, decay = [], [], [], [], []
    for c in range(nch):
        sl = slice(c * L, (c + 1) * L)
        bc = b[sl]
        mc = bc[L // 2:L // 2 + 1]
        blc = bc[L - 1:L] if dirn == 0 else bc[0:1]
        qe.append((q[sl] * jnp.exp(bc - mc)).astype(BF16))
        ke.append((k[sl] * jnp.exp(mc - bc)).astype(BF16))
        qd.append((q[sl] * jnp.exp(bc)).astype(BF16))
        kd.append((k[sl] * jnp.exp(blc - bc)).astype(BF16))
        decay.append(jnp.exp(blc))
    attn = _dot_nt(jnp.concatenate(qe, axis=0), jnp.concatenate(ke, axis=0))
    intra = _dot(jnp.where(tri, attn, 0.0).astype(BF16), v)
    for c in (range(nch) if dirn == 0 else range(nch - 1, -1, -1)):
        sl = slice(c * L, (c + 1) * L)
        st = st_ref[dirn, hh]
        o_ref[sl, v_cols] = (intra[sl] + _dot_nt(qd[c], st.astype(BF16))).astype(o_ref.dtype)
        upd = lax.dot_general(v[sl], kd[c], (((0,), (0,)), ((), ())), preferred_element_type=F32)
        st_ref[dirn, hh] = decay[c] * st + upd


GLA_HPS = 2


def _gla_scan_kernel(dk, qf, kf, vf, rf, cosf, sinf, qb, kb, vb, rb, cosb, sinb, w2_ref, ba_ref, of_ref, ob_ref,
                     st_ref):
    @pl.when(pl.program_id(2) == 0)
    def _():
        st_ref[...] = jnp.zeros_like(st_ref)

    for hh in range(GLA_HPS):
        _gla_block(dk, 0, hh, qf, kf, vf, rf, cosf, sinf, w2_ref, ba_ref, of_ref, st_ref)
        _gla_block(dk, 1, hh, qb, kb, vb, rb, cosb, sinb, w2_ref, ba_ref, ob_ref, st_ref)


def _gla_scan(z, r, w2pad, ba, cos, sin, batch, s_lat, c_ctx, d):
    n = z.shape[0]
    hg = GLA_HEADS
    dkt = d // 2
    dk, dv = dkt // hg, d // hg
    nl = s_lat // TM
    nt = nl + 1
    assert c_ctx == TM

    blks = (lambda s: jnp.where(s == 0, nl, s - 1), lambda s: jnp.where(s == 0, nl, nl - s))
    assert hg % GLA_HPS == 0
    ng = hg // GLA_HPS
    wk, wv = GLA_HPS * dk, GLA_HPS * dv
    in_specs = []
    for blk in blks:
        in_specs += [
            pl.BlockSpec((TM, wk), lambda b, h, s, blk=blk: (b * nt + blk(s), h)),
            pl.BlockSpec((TM, wk), lambda b, h, s, blk=blk: (b * nt + blk(s), ng + h)),
            pl.BlockSpec((TM, wv), lambda b, h, s, blk=blk: (b * nt + blk(s), 2 * dkt // wv + h)),
            pl.BlockSpec((TM, LANES), lambda b, h, s, blk=blk: (b * nt + blk(s), 0)),
            pl.BlockSpec((TM, dk), lambda b, h, s, blk=blk: (blk(s), 0)),
            pl.BlockSpec((TM, dk), lambda b, h, s, blk=blk: (blk(s), 0)),
        ]
    in_specs += [
        pl.BlockSpec((2, GLA_HPS, 3 * LANES, dk), lambda b, h, s: (0, h, 0, 0)),
        pl.BlockSpec((2, GLA_HPS, 1, dk), lambda b, h, s: (0, h, 0, 0)),
    ]
    out_specs = [pl.BlockSpec((TM, wv), lambda b, h, s, blk=blk: (b * nt + blk(s), h)) for blk in blks]
    return pl.pallas_call(
        functools.partial(_gla_scan_kernel, dk),
        grid=(batch, ng, nt),
        in_specs=in_specs,
        out_specs=out_specs,
        out_shape=[jax.ShapeDtypeStruct((n, d), BF16), jax.ShapeDtypeStruct((n, d), BF16)],
        scratch_shapes=[pltpu.VMEM((2, GLA_HPS, dv, dk), F32)],
        compiler_params=_cp("arbitrary", "arbitrary", "arbitrary"),
        name="gla_scan",
    )(z, z, z, r, cos, sin, z, z, z, r, cos, sin, w2pad, ba)


def _gla_out_kernel(dv, of_ref, ob_ref, g_ref, nw_ref, h_ref):
    nw = nw_ref[0]
    for h in range(of_ref.shape[1] // dv):
        sl = slice(h * dv, (h + 1) * dv)
        o = of_ref[:, sl].astype(F32) + ob_ref[:, sl].astype(F32)
        o = o * lax.rsqrt(jnp.mean(o * o, axis=-1, keepdims=True) + LN_EPS) * nw
        h_ref[:, sl] = (o * _silu(g_ref[:, sl].astype(F32))).astype(h_ref.dtype)


def _gla_out(o_f, o_b, z, norm_w, layer, d):
    n = z.shape[0]
    dv = d // GLA_HEADS
    gcol = (z.shape[1] - d) // d
    row = pl.BlockSpec((TM, d), lambda i: (i, 0))
    return pl.pallas_call(
        functools.partial(_gla_out_kernel, dv),
        grid=(n // TM,),
        in_specs=[
            row, row,
            pl.BlockSpec((TM, d), lambda i: (i, gcol)),
            pl.BlockSpec((1, 1, dv), lambda i: (layer, 0, 0)),
        ],
        out_specs=row,
        out_shape=jax.ShapeDtypeStruct((n, d), BF16),
        compiler_params=_cp("arbitrary"),
        name="gla_out",
    )(o_f, o_b, z, norm_w.reshape(norm_w.shape[0], 1, dv))


def _rank_desc(v, axis_len):
    idx = lax.broadcasted_iota(jnp.int32, v.shape, 0)
    rank = jnp.zeros(v.shape, jnp.int32)
    for j in range(axis_len):
        vj = v[j:j + 1]
        rank = rank + jnp.where((vj > v) | ((vj == v) & (j < idx)), 1, 0)
    return rank


def _router_kernel(n_e, x_ref, sh_ref, sc_ref, wr_ref, eb_ref, eid_ref, rnk_ref, gw_ref, cnt_ref, run_ref):
    @pl.when(pl.program_id(0) == 0)
    def _():
        run_ref[...] = jnp.zeros_like(run_ref)

    tm = x_ref.shape[0]
    gsz = n_e // N_GROUPS
    h = x_ref[...] * (1.0 + sc_ref[0]) + sh_ref[0]
    logits = lax.dot_general(wr_ref[0], h, (((1,), (1,)), ((), ())), precision=HI, preferred_element_type=F32)
    scores = _sigmoid(logits)
    biased = scores + eb_ref[0]
    grp = biased.reshape(N_GROUPS, gsz, tm)
    m1 = jnp.max(grp, axis=1, keepdims=True)
    gi = lax.broadcasted_iota(jnp.int32, grp.shape, 1)
    first = jnp.min(jnp.where(grp == m1, gi, gsz), axis=1, keepdims=True)
    m2 = jnp.max(jnp.where(gi == first, -jnp.inf, grp), axis=1, keepdims=True)
    g_sel = _rank_desc(m1 + m2, N_GROUPS) < TOPK_GROUPS
    e_sel = jnp.broadcast_to(g_sel, (N_GROUPS, gsz, tm)).reshape(n_e, tm)
    masked = jnp.where(e_sel, biased, NEG_INF)
    eidx_i = lax.broadcasted_iota(jnp.int32, (n_e, tm), 0)
    firsts, hits = [], []
    sel_f = jnp.zeros((n_e, tm), F32)
    for _ in range(TOP_K):
        mx = jnp.max(masked, axis=0, keepdims=True)
        first = jnp.min(jnp.where(masked == mx, eidx_i, n_e), axis=0, keepdims=True)
        hit = eidx_i == first
        firsts.append(first)
        hits.append(hit)
        sel_f = sel_f + jnp.where(hit, 1.0, 0.0)
        masked = jnp.where(hit, -jnp.inf, masked)
    w = scores * sel_f
    gates = w / jnp.sum(w, axis=0, keepdims=True) * ROUTED_SCALE
    t0 = lax.broadcasted_iota(jnp.int32, (tm, tm), 0)
    t1 = lax.broadcasted_iota(jnp.int32, (tm, tm), 1)
    before = _dot(sel_f.astype(BF16), jnp.where(t0 < t1, 1.0, 0.0).astype(BF16))
    pos = run_ref[:, 0:1] + before
    run_ref[...] = run_ref[...] + jnp.sum(sel_f, axis=1, keepdims=True)
    cnt_ref[...] = run_ref[...].astype(jnp.int32)
    for j in range(TOP_K):
        eid_ref[j:j + 1, :] = firsts[j]
        rnk_ref[j:j + 1, :] = jnp.sum(jnp.where(hits[j], pos, 0.0), axis=0, keepdims=True).astype(jnp.int32)
        gw_ref[j:j + 1, :] = jnp.sum(jnp.where(hits[j], gates, 0.0), axis=0, keepdims=True)


def _router(x, shift, scale, w_router_t, e_bias, layer, seg):
    n, d = x.shape
    n_e = w_router_t.shape[1]
    vec = pl.BlockSpec((1, 1, d), lambda i: (seg(i), 0, 0))
    sel = pl.BlockSpec((TOP_K, TM), lambda i: (0, i))
    return pl.pallas_call(
        functools.partial(_router_kernel, n_e),
        grid=(n // TM,),
        in_specs=[
            pl.BlockSpec((TM, d), lambda i: (i, 0)),
            vec, vec,
            pl.BlockSpec((1, n_e, d), lambda i: (layer, 0, 0)),
            pl.BlockSpec((1, n_e, 1), lambda i: (layer, 0, 0)),
        ],
        out_specs=[sel, sel, sel, pl.BlockSpec((n_e, LANES), lambda i: (0, 0))],
        out_shape=[
            jax.ShapeDtypeStruct((TOP_K, n), jnp.int32),
            jax.ShapeDtypeStruct((TOP_K, n), jnp.int32),
            jax.ShapeDtypeStruct((TOP_K, n), F32),
            jax.ShapeDtypeStruct((n_e, LANES), jnp.int32),
        ],
        scratch_shapes=[pltpu.VMEM((n_e, LANES), F32)],
        compiler_params=_cp("arbitrary"),
        name="router",
    )(x, shift, scale, w_router_t, e_bias)


ROW_UNROLL = 8
GROUP_UNROLL = 8


def _gather_rows(idx_ref, src_ref, dst_ref, sem, n_rows):
    def group(g, c):
        for v in range(GROUP_UNROLL):
            for u in range(ROW_UNROLL):
                t = idx_ref[0, 0, (g * GROUP_UNROLL + v) * ROW_UNROLL + u]
                src = src_ref.at[lax.shift_right_logical(t, 3), pl.ds(t & 7, 1)]
                pltpu.make_async_copy(src, dst_ref.at[g * GROUP_UNROLL + v, pl.ds(u, 1)], sem).start(priority=u % 2)
        return c

    lax.fori_loop(0, n_rows // (ROW_UNROLL * GROUP_UNROLL), group, 0)


def _wait_rows(src_ref, dst_ref, sem, n_rows):
    pltpu.make_async_copy(src_ref.at[pl.ds(0, n_rows // ROW_UNROLL)], dst_ref, sem).wait()


GATHER_AHEAD = 3


def _request_rows(step, n_valid, first_refs, ahead_ref, src_ref, buf, sem, n_rows):
    for a, ref in enumerate(first_refs):
        @pl.when((step == 0) & (n_valid > a))
        def _(a=a, ref=ref):
            _gather_rows(ref, src_ref, buf.at[a], sem.at[a], n_rows)

    @pl.when(step + GATHER_AHEAD < n_valid)
    def _():
        s = lax.rem(step + GATHER_AHEAD, GATHER_AHEAD + 1)
        _gather_rows(ahead_ref, src_ref, buf.at[s], sem.at[s], n_rows)


def _expert_kernel(te_ref, nt_ref, *refs):
    first_refs = refs[:GATHER_AHEAD]
    tokn_ref, hp_ref, wg_ref, wu_ref, wd_ref, y_ref, xbuf, wgb, wub, wdb, sem = refs[GATHER_AHEAD:]
    j = pl.program_id(0)
    nt = nt_ref[0]
    tg = xbuf.shape[1] * xbuf.shape[2]
    slot = lax.rem(j, GATHER_AHEAD + 1)
    _request_rows(j, nt, first_refs, tokn_ref, hp_ref, xbuf, sem, tg)

    @pl.when(j < nt)
    def _():
        @pl.when((j == 0) | (te_ref[j] != te_ref[jnp.maximum(j - 1, 0)]))
        def _():
            wgb[...] = wg_ref[0, 0].astype(BF16)
            wub[...] = wu_ref[0, 0].astype(BF16)
            wdb[...] = wd_ref[0, 0].astype(BF16)

        _wait_rows(hp_ref, xbuf.at[slot], sem.at[slot], tg)
        lo, hi = _unpack_rows(xbuf[slot].reshape(tg, xbuf.shape[3]))
        x = jnp.concatenate([lo.astype(BF16), hi.astype(BF16)], axis=1)
        hid = _silu(_dot_nt(x, wgb[...])) * _dot_nt(x, wub[...])
        y_ref[...] = _pack_rows(_dot(hid.astype(BF16), wdb[...]))

    @pl.when(j >= nt)
    def _():
        y_ref[...] = jnp.zeros_like(y_ref)


def _experts(hp, tok, tile_eid, n_tiles_used, w_gate_t, w_up_t, w_down, layer):
    n, dh = hp.shape
    d = 2 * dh
    f = w_gate_t.shape[2]
    n_tiles = tok.shape[0]
    assert n % ROW_UNROLL == 0 and TG % ROW_UNROLL == 0
    grid_spec = pltpu.PrefetchScalarGridSpec(
        num_scalar_prefetch=2,
        grid=(n_tiles,),
        in_specs=[
            pl.BlockSpec((1, 1, TG), lambda j, te, nt, a=a: (a, 0, 0), memory_space=pltpu.SMEM)
            for a in range(GATHER_AHEAD)
        ] + [
            pl.BlockSpec((1, 1, TG), lambda j, te, nt: (jnp.minimum(j + GATHER_AHEAD, n_tiles - 1), 0, 0),
                         memory_space=pltpu.SMEM),
            pl.BlockSpec(memory_space=pl.ANY),
            pl.BlockSpec((1, 1, f, d), lambda j, te, nt: (layer, te[j], 0, 0)),
            pl.BlockSpec((1, 1, f, d), lambda j, te, nt: (layer, te[j], 0, 0)),
            pl.BlockSpec((1, 1, f, d), lambda j, te, nt: (layer, te[j], 0, 0)),
        ],
        out_specs=pl.BlockSpec((TG, dh), lambda j, te, nt: (j, 0)),
        scratch_shapes=[
            pltpu.VMEM((GATHER_AHEAD + 1, TG // ROW_UNROLL, ROW_UNROLL, dh), U32),
            pltpu.VMEM((f, d), BF16),
            pltpu.VMEM((f, d), BF16),
            pltpu.VMEM((f, d), BF16),
            pltpu.SemaphoreType.DMA((GATHER_AHEAD + 1,)),
        ],
    )
    return pl.pallas_call(
        _expert_kernel,
        grid_spec=grid_spec,
        out_shape=jax.ShapeDtypeStruct((n_tiles * TG, dh), U32),
        compiler_params=_cp("arbitrary"),
        name="experts",
    )(tile_eid, n_tiles_used, *([tok] * (GATHER_AHEAD + 1)), hp.reshape(n // ROW_UNROLL, ROW_UNROLL, dh),
      w_gate_t, w_up_t, w_down)


def _shared_kernel(hp_ref, sg_ref, su_ref, sd_ref, y_ref, sgb, sub, sdb):
    @pl.when(pl.program_id(0) == 0)
    def _():
        sgb[...] = sg_ref[0].astype(BF16)
        sub[...] = su_ref[0].astype(BF16)
        sdb[...] = sd_ref[0].astype(BF16)

    lo, hi = _unpack_rows(hp_ref[...])
    x = jnp.concatenate([lo.astype(BF16), hi.astype(BF16)], axis=1)
    hid = _silu(_dot_nt(x, sgb[...])) * _dot_nt(x, sub[...])
    y_ref[...] = _dot(hid.astype(BF16), sdb[...]).astype(y_ref.dtype)


def _shared_expert(hp, sg_t, su_t, sd, layer):
    n, dh = hp.shape
    d = 2 * dh
    f = sg_t.shape[1]
    tm = _row_tile(n)
    wspec = pl.BlockSpec((1, f, d), lambda i: (layer, 0, 0))
    return pl.pallas_call(
        _shared_kernel,
        grid=(n // tm,),
        in_specs=[pl.BlockSpec((tm, dh), lambda i: (i, 0)), wspec, wspec, wspec],
        out_specs=pl.BlockSpec((tm, d), lambda i: (i, 0)),
        out_shape=jax.ShapeDtypeStruct((n, d), BF16),
        scratch_shapes=[pltpu.VMEM((f, d), BF16), pltpu.VMEM((f, d), BF16), pltpu.VMEM((f, d), BF16)],
        compiler_params=_cp("arbitrary"),
        name="shared_expert",
    )(hp, sg_t, su_t, sd)


TMC = 128


def _combine_kernel(alpha, *refs):
    first_refs = refs[:GATHER_AHEAD]
    (destn_ref, ys_ref, ysh_ref, gw_ref, x_ref, g_ref, lng_ref, lnb_ref, sh_ref, sc_ref, xo_ref, h_ref, buf,
     sem) = refs[GATHER_AHEAD:]
    tm, d = x_ref.shape
    dh = d // 2
    i = pl.program_id(0)
    slot = lax.rem(i, GATHER_AHEAD + 1)
    n_rows = TOP_K * tm
    _request_rows(i, pl.num_programs(0), first_refs, destn_ref, ys_ref, buf, sem, n_rows)
    _wait_rows(ys_ref, buf.at[slot], sem.at[slot], n_rows)
    ysh = ysh_ref[...].astype(F32)
    y_lo, y_hi = ysh[:, :dh], ysh[:, dh:]
    tg = tm // ROW_UNROLL
    for kk in range(TOP_K):
        lo, hi = _unpack_rows(buf[slot, kk * tg:(kk + 1) * tg].reshape(tm, dh))
        w = gw_ref[:, kk:kk + 1]
        y_lo = y_lo + w * lo
        y_hi = y_hi + w * hi
    y = jnp.concatenate([y_lo, y_hi], axis=1)
    v = alpha * x_ref[...] + g_ref[0] * y
    xn, h = _ln_mod(v, lng_ref[0], lnb_ref[0], sh_ref[0], sc_ref[0])
    xo_ref[...] = xn
    h_ref[...] = h.astype(h_ref.dtype)


def _combine(dest, ys, ysh, gw, x, gate, lng, lnb, shift, scale, seg, alpha):
    n, d = x.shape
    row = pl.BlockSpec((TMC, d), lambda i: (i, 0))
    vec = pl.BlockSpec((1, 1, d), lambda i: (seg(i), 0, 0))
    one = pl.BlockSpec((1, 1, d), lambda i: (0, 0, 0))
    n_tiles = n // TMC
    return pl.pallas_call(
        functools.partial(_combine_kernel, alpha),
        grid=(n_tiles,),
        in_specs=[
            pl.BlockSpec((1, 1, TOP_K * TMC), lambda i, a=a: (a, 0, 0), memory_space=pltpu.SMEM)
            for a in range(GATHER_AHEAD)
        ] + [
            pl.BlockSpec((1, 1, TOP_K * TMC), lambda i: (jnp.minimum(i + GATHER_AHEAD, n_tiles - 1), 0, 0),
                         memory_space=pltpu.SMEM),
            pl.BlockSpec(memory_space=pl.ANY),
            row,
            pl.BlockSpec((TMC, TOP_K), lambda i: (i, 0)),
            row, vec, one, one, vec, vec,
        ],
        out_specs=[row, row],
        out_shape=[jax.ShapeDtypeStruct((n, d), F32), jax.ShapeDtypeStruct((n, d), BF16)],
        scratch_shapes=[pltpu.VMEM((GATHER_AHEAD + 1, TOP_K * TMC // ROW_UNROLL, ROW_UNROLL, d // 2), U32),
                        pltpu.SemaphoreType.DMA((GATHER_AHEAD + 1,))],
        compiler_params=_cp("arbitrary"),
        name="moe_combine",
    )(*([dest] * (GATHER_AHEAD + 1)), ys.reshape(ys.shape[0] // ROW_UNROLL, ROW_UNROLL, d // 2), ysh, gw, x, gate, lng, lnb, shift, scale)


def _moe(x, hp, mods_f, w_router_t, e_bias, w_gate, w_up, w_down, sg, su, sd, gate, lng, lnb, nshift, nscale,
         layer, seg, seg_c, alpha):
    n, d = x.shape
    n_e = w_router_t.shape[1]
    shift_f, scale_f = mods_f
    eid, rnk, gw, cnt = _router(x, shift_f, scale_f, w_router_t, e_bias, layer, seg)
    counts = cnt[:, 0]
    tiles_e = (counts + TG - 1) // TG
    tile_end = jnp.cumsum(tiles_e)
    n_tiles = (n * TOP_K) // TG + n_e
    tile_ids = jnp.arange(n_tiles, dtype=jnp.int32)
    tile_eid = jnp.minimum(jnp.sum((tile_end[None, :] <= tile_ids[:, None]).astype(jnp.int32), axis=1), n_e - 1)
    row_off = (tile_end - tiles_e) * TG
    e_ids = jnp.arange(n_e, dtype=jnp.int32)
    dest = jnp.sum(jnp.where(eid[..., None] == e_ids, row_off, 0), axis=-1) + rnk
    tok = jnp.zeros((n_tiles * TG,), jnp.int32).at[dest.reshape(-1)].set(
        jnp.broadcast_to(jnp.arange(n, dtype=jnp.int32), (TOP_K, n)).reshape(-1),
        unique_indices=True, mode="promise_in_bounds")
    ys = _experts(hp, tok.reshape(n_tiles, 1, TG), tile_eid, tile_end[-1:].astype(jnp.int32), w_gate, w_up, w_down,
                  layer)
    ysh = _shared_expert(hp, sg, su, sd, layer)
    dest_t = dest.reshape(TOP_K, n // TMC, TMC).transpose(1, 0, 2).reshape(n // TMC, 1, TOP_K * TMC)
    return _combine(dest_t, ys, ysh, gw.T, x, gate, lng, lnb, nshift, nscale, seg_c, alpha)


def kernel(x, c, ctx, c_ctx, ada_w1, ada_w2, ada_b, ln_g, ln_b, na_w_qkv, na_w_o, na_rpb, gla_w_in, gla_w_a1,
           gla_w_a2, gla_b_a, gla_norm, gla_w_o, moe_router, moe_bias, moe_w_gate, moe_w_up, moe_w_down,
           sh_w_gate, sh_w_up, sh_w_down):
    batch, s_lat, d = x.shape
    c_ctx_len = ctx.shape[1]
    depth = ada_w1.shape[0]
    alpha = (2 * depth) ** 0.25
    assert s_lat % TM == 0 and c_ctx_len == TM and s_lat % GRID_W == 0
    r = s_lat + c_ctx_len
    n = batch * r
    nl, nt = s_lat // TM, r // TM
    seg = _seg_fn(nt, nl)
    seg_c = _seg_fn(nt * (TM // TMC), nl * (TM // TMC))

    xa = jnp.concatenate([x, ctx], axis=1).reshape(n, d)
    cond = jnp.stack([c, jnp.broadcast_to(c_ctx, c.shape)], axis=1).reshape(2 * batch, d)
    rows = -(-2 * batch // 8) * 8
    cond = jnp.pad(cond, ((0, rows - 2 * batch), (0, 0)))
    mods = _adaln(cond, ada_w1, ada_w2, ada_b).reshape(depth, rows, 6, d).transpose(0, 2, 1, 3)
    mods = mods.reshape(depth, 6, rows, 1, d)
    lng = ln_g.reshape(depth, 2, 1, 1, d)
    lnb = ln_b.reshape(depth, 2, 1, 1, d)

    hd_gla = d // 2 // GLA_HEADS
    cos, sin = _rope_tables(s_lat, c_ctx_len, hd_gla)
    n_e = moe_router.shape[2]
    w_router_t = jnp.swapaxes(moe_router, 1, 2)
    e_bias = moe_bias.reshape(depth, n_e, 1)
    w_gate_t, w_up_t = jnp.swapaxes(moe_w_gate, 2, 3), jnp.swapaxes(moe_w_up, 2, 3)
    sg_t, su_t = jnp.swapaxes(sh_w_gate, 1, 2), jnp.swapaxes(sh_w_up, 1, 2)
    rank_g = gla_w_a1.shape[3]
    assert 2 * rank_g <= LANES

    h = _modulate(xa, mods[0, 0], mods[0, 1], seg)
    for i in range(depth):
        last = i == depth - 1
        m = mods[i]
        j = i // 2
        if i % 2 == 0:
            qkv = _mm(h, na_w_qkv, j, BF16, 512)
            o = _na_attention(qkv, na_rpb[j], batch, s_lat, c_ctx_len, d)
            y = _mm(o, na_w_o, j, BF16, 512)
        else:
            z = _mm(h, gla_w_in, j, BF16, 512)
            w_a1 = jnp.concatenate([gla_w_a1[j, 0], gla_w_a1[j, 1]], axis=1)
            w_a1 = jnp.pad(w_a1, ((0, 0), (0, LANES - 2 * rank_g)))[None]
            rr = _mm(h, w_a1, 0, F32, LANES)
            w2 = gla_w_a2[j].reshape(2, rank_g, GLA_HEADS, hd_gla).transpose(0, 2, 1, 3)
            w2pad = jnp.zeros((2, GLA_HEADS, LANES, hd_gla), F32)
            w2pad = w2pad.at[0, :, :rank_g].set(w2[0]).at[1, :, rank_g:2 * rank_g].set(w2[1])
            w2_hi = w2pad.astype(BF16)
            w2_lo = (w2pad - w2_hi.astype(F32)).astype(BF16)
            w2cat = jnp.concatenate([w2_hi, w2_hi, w2_lo], axis=2)
            ba = gla_b_a[j].reshape(2, GLA_HEADS, 1, hd_gla)
            o_f, o_b = _gla_scan(z, rr, w2cat, ba, cos, sin, batch, s_lat, c_ctx_len, d)
            og = _gla_out(o_f, o_b, z, gla_norm, j, d)
            y = _mm(og, gla_w_o, j, BF16, 512)
        x1, hp = _resid_ln(xa, y, m[2], lng[i, 0], lnb[i, 0], m[3], m[4], seg, alpha, packed=True)
        nxt = mods[i + 1] if not last else m
        xa, h = _moe(x1, hp, (m[3], m[4]), w_router_t, e_bias, w_gate_t, w_up_t, moe_w_down,
                     sg_t, su_t, sh_w_down, m[5], lng[i, 1], lnb[i, 1], nxt[0], nxt[1],
                     i, seg, seg_c, alpha)
    return xa.reshape(batch, r, d)[:, :s_lat]
```
